```python
import math
import jax, jax.numpy as jnp
from jax import lax
import numpy as np

D_MODEL = 2048
BATCH = 2
SEQ = 8192
DEPTH = 2

F32 = jnp.float32
EPS = 1e-6
GRID_W = 64

S5_WIDTH = D_MODEL // 2
S5_GROUP = 16
S5_GROUPS = S5_WIDTH // S5_GROUP
S5_STATE = 64
DT_MIN = 1e-3
DT_MAX = 1e-1

HEAD_DIM = 128
N_Q_HEADS = (D_MODEL // 2) // HEAD_DIM
N_KV_HEADS = max(1, N_Q_HEADS // 4)
GQA_GROUP = N_Q_HEADS // N_KV_HEADS
ATTN_WIDTH = N_Q_HEADS * HEAD_DIM
ROPE_THETA = 10000.0
BLOCK_Q = 128

IN_A = S5_WIDTH
IN_Q = ATTN_WIDTH
IN_KV = N_KV_HEADS * HEAD_DIM
EVEN_IN = IN_A + IN_Q + 2 * IN_KV
EVEN_OUT = S5_WIDTH + ATTN_WIDTH

SGU_CHUNK = 128
SGU_GROUPS = 8
SGU_WIDTH = 2 * D_MODEL
SGU_GROUP_W = SGU_WIDTH // SGU_GROUPS

PEER_HEADS = 8
PEER_NKEYS = 128
PEER_EXPERTS = PEER_NKEYS * PEER_NKEYS
PEER_TOPK = 16
PEER_DKEY = 256
PEER_BLOCK = 128

N_EVEN = (DEPTH + 1) // 2
N_ODD = DEPTH // 2

kernel_name = "hybrid_s5_gqa_sgu_peer_encoder"


def rms_norm(x, g):
    x32 = x.astype(F32)
    y = x32 * lax.rsqrt(jnp.mean(x32 * x32, axis=-1, keepdims=True) + EPS)
    return (y * g.astype(F32)).astype(x.dtype)


def ada_rms_norm(x, g, shift, scale):
    return rms_norm(x, g) * (1 + scale[:, None, :]) + shift[:, None, :]


def axial_rope(seq_len):
    rows = seq_len // GRID_W
    row = jnp.repeat(jnp.arange(rows, dtype=F32), GRID_W)
    col = jnp.tile(jnp.arange(GRID_W, dtype=F32), rows)
    n_pair_axis = HEAD_DIM // 4
    inv_freq = ROPE_THETA ** (-jnp.arange(n_pair_axis, dtype=F32) / n_pair_axis)
    ang = jnp.concatenate([row[:, None] * inv_freq, col[:, None] * inv_freq], axis=-1)
    return jnp.cos(ang), jnp.sin(ang)


def apply_rope(x, cos, sin):
    x32 = x.astype(F32).reshape(x.shape[:-1] + (HEAD_DIM // 2, 2))
    x1, x2 = x32[..., 0], x32[..., 1]
    c = cos[None, :, None, :]
    s = sin[None, :, None, :]
    out = jnp.stack([x1 * c - x2 * s, x1 * s + x2 * c], axis=-1).reshape(x.shape)
    return out.astype(x.dtype)


def block_attention(q, k, v):
    bsz, hkv, grp, s_len, dh = q.shape
    nb = s_len // BLOCK_Q
    qb = q.reshape(bsz, hkv, grp, nb, BLOCK_Q, dh).transpose(3, 0, 1, 2, 4, 5)
    scale = dh ** -0.5

    def one_block(qblk):
        s = jnp.einsum('bhgqd,bhkd->bhgqk', qblk, k, preferred_element_type=F32) * scale
        p = jax.nn.softmax(s, axis=-1)
        return jnp.einsum('bhgqk,bhkd->bhgqd', p.astype(v.dtype), v)

    o = lax.map(one_block, qb)
    return o.transpose(1, 0, 4, 2, 3, 5).reshape(bsz, s_len, hkv * grp * dh)


def _ssm_combine(e1, e2):
    a1r, a1i, b1r, b1i = e1
    a2r, a2i, b2r, b2i = e2
    return (a2r * a1r - a2i * a1i,
            a2r * a1i + a2i * a1r,
            a2r * b1r - a2i * b1i + b2r,
            a2r * b1i + a2i * b1r + b2i)


def s5_mixer(u, lam_re, lam_im, log_dt, b_re, b_im, c_re, c_im, d_skip, glu_w, glu_b):
    bsz, s_len, n_g, _ = u.shape
    u32 = u.astype(F32)
    y = d_skip.astype(F32) * u32
    for direction in range(2):
        lr = jnp.minimum(lam_re[direction].astype(F32), -1e-4)
        li = lam_im[direction].astype(F32)
        dt = jnp.exp(log_dt[direction].astype(F32))[:, None]
        mag = jnp.exp(lr * dt)
        ab_re = mag * jnp.cos(li * dt)
        ab_im = mag * jnp.sin(li * dt)
        den = lr * lr + li * li
        nr = ab_re - 1.0
        coef_re = (nr * lr + ab_im * li) / den
        coef_im = (ab_im * lr - nr * li) / den
        br = b_re[direction].astype(F32)
        bi = b_im[direction].astype(F32)
        bb_re = coef_re[..., None] * br - coef_im[..., None] * bi
        bb_im = coef_re[..., None] * bi + coef_im[..., None] * br
        x_re = jnp.einsum('bsgh,gph->bsgp', u32, bb_re)
        x_im = jnp.einsum('bsgh,gph->bsgp', u32, bb_im)
        a_re = jnp.broadcast_to(ab_re[None, None], (1, s_len) + ab_re.shape)
        a_im = jnp.broadcast_to(ab_im[None, None], (1, s_len) + ab_im.shape)
        _, _, h_re, h_im = lax.associative_scan(
            _ssm_combine, (a_re, a_im, x_re, x_im), reverse=(direction == 1), axis=1)
        y = y + jnp.einsum('bsgp,ghp->bsgh', h_re, c_re[direction].astype(F32)) \
              - jnp.einsum('bsgp,ghp->bsgh', h_im, c_im[direction].astype(F32))
    y = jax.nn.gelu(y)
    gate = jax.nn.sigmoid(jnp.einsum('bsgh,ghk->bsgk', y, glu_w.astype(F32)) + glu_b.astype(F32))
    return (y * gate).astype(u.dtype)


def even_mixer(h, w_in, w_out, lam_re, lam_im, log_dt, b_re, b_im, c_re, c_im, d_skip,
               glu_w, glu_b, q_g, k_g, cos, sin):
    bsz, s_len, _ = h.shape
    z = h @ w_in
    ua, q, k, v = jnp.split(z, [IN_A, IN_A + IN_Q, IN_A + IN_Q + IN_KV], axis=-1)
    a_out = s5_mixer(ua.reshape(bsz, s_len, S5_GROUPS, S5_GROUP), lam_re, lam_im, log_dt,
                     b_re, b_im, c_re, c_im, d_skip, glu_w, glu_b).reshape(bsz, s_len, S5_WIDTH)
    q = apply_rope(rms_norm(q.reshape(bsz, s_len, N_Q_HEADS, HEAD_DIM), q_g), cos, sin)
    k = apply_rope(rms_norm(k.reshape(bsz, s_len, N_KV_HEADS, HEAD_DIM), k_g), cos, sin)
    v = v.reshape(bsz, s_len, N_KV_HEADS, HEAD_DIM)
    q = q.reshape(bsz, s_len, N_KV_HEADS, GQA_GROUP, HEAD_DIM).transpose(0, 2, 3, 1, 4)
    b_out = block_attention(q, k.transpose(0, 2, 1, 3), v.transpose(0, 2, 1, 3))
    return jnp.concatenate([a_out, b_out], axis=-1) @ w_out


def odd_mixer(h, w_in, w_out, ln_g, w_s, b_s):
    bsz, s_len, _ = h.shape
    z = jax.nn.gelu(h @ w_in)
    u, v = jnp.split(z, 2, axis=-1)
    v32 = v.reshape(bsz, s_len // SGU_CHUNK, SGU_CHUNK, SGU_GROUPS, SGU_GROUP_W).astype(F32)
    mu = jnp.mean(v32, axis=-1, keepdims=True)
    var = jnp.mean(jnp.square(v32 - mu), axis=-1, keepdims=True)
    vn = ((v32 - mu) * lax.rsqrt(var + EPS) * ln_g.astype(F32)).astype(h.dtype)
    s = jnp.einsum('gij,bnjgc->bnigc', w_s, vn) + b_s.T[None, None, :, :, None]
    return (u * s.reshape(bsz, s_len, SGU_WIDTH)) @ w_out


def peer(h, w_q, sub_k1, sub_k2, u_tab, v_tab):
    bsz, s_len, d = h.shape
    t = bsz * s_len
    ht = h.reshape(t, d)
    q = (ht @ w_q).reshape(t, PEER_HEADS, 2, PEER_DKEY // 2)
    s1 = jnp.einsum('thd,kd->thk', q[:, :, 0], sub_k1, preferred_element_type=F32)
    s2 = jnp.einsum('thd,kd->thk', q[:, :, 1], sub_k2, preferred_element_type=F32)
    v1, i1 = lax.top_k(s1, PEER_TOPK)
    v2, i2 = lax.top_k(s2, PEER_TOPK)
    cand = (v1[..., :, None] + v2[..., None, :]).reshape(t, PEER_HEADS, PEER_TOPK * PEER_TOPK)
    sc, ci = lax.top_k(cand, PEER_TOPK)
    r1 = jnp.take_along_axis(i1, ci // PEER_TOPK, axis=-1)
    r2 = jnp.take_along_axis(i2, ci % PEER_TOPK, axis=-1)
    eidx = r1 * PEER_NKEYS + r2
    gates = jax.nn.softmax(sc, axis=-1)
    nb = t // PEER_BLOCK
    xb = ht.reshape(nb, PEER_BLOCK, d)
    ib = eidx.reshape(nb, PEER_BLOCK, PEER_HEADS * PEER_TOPK)
    gb = gates.reshape(nb, PEER_BLOCK, PEER_HEADS * PEER_TOPK)

    def one_block(args):
        xblk, iblk, gblk = args
        act = jnp.einsum('td,tkd->tk', xblk, u_tab[iblk], preferred_element_type=F32)
        w = (jax.nn.gelu(act) * gblk).astype(xblk.dtype)
        return jnp.einsum('tk,tkd->td', w, v_tab[iblk])

    out = lax.map(one_block, (xb, ib, gb))
    return out.reshape(bsz, s_len, d)


def setup_inputs(seed: int = 0) -> dict:
    key = jax.random.key(seed)
    k = jax.random.split(key, 32)

    def nrm(i, shape, std):
        return jax.random.normal(k[i], shape, F32) * std

    D = D_MODEL
    G, P, H = S5_GROUPS, S5_STATE, S5_GROUP
    lam_im_base = jnp.pi * jnp.arange(P, dtype=F32)
    return {
        "x": nrm(0, (BATCH, SEQ, D), 1.0),
        "c": nrm(1, (BATCH, D), 1.0),
        "ada_w": nrm(2, (DEPTH, D, 6 * D), 0.5 * D ** -0.5),
        "ada_b": nrm(3, (DEPTH, 6 * D), 0.01),
        "norm_g": 1.0 + nrm(4, (DEPTH, 2, D), 0.01),
        "even_w_in": nrm(5, (N_EVEN, D, EVEN_IN), D ** -0.5),
        "even_w_out": nrm(6, (N_EVEN, EVEN_OUT, D), EVEN_OUT ** -0.5),
        "s5_lam_re": -0.5 + nrm(7, (N_EVEN, 2, G, P), 0.01),
        "s5_lam_im": lam_im_base + nrm(8, (N_EVEN, 2, G, P), 0.01),
        "s5_log_dt": jax.random.uniform(k[9], (N_EVEN, 2, G), F32,
                                        minval=math.log(DT_MIN), maxval=math.log(DT_MAX)),
        "s5_b_re": nrm(10, (N_EVEN, 2, G, P, H), (2 * H) ** -0.5),
        "s5_b_im": nrm(11, (N_EVEN, 2, G, P, H), (2 * H) ** -0.5),
        "s5_c_re": nrm(12, (N_EVEN, 2, G, H, P), 0.5 ** 0.5),
        "s5_c_im": nrm(13, (N_EVEN, 2, G, H, P), 0.5 ** 0.5),
        "s5_d": nrm(14, (N_EVEN, G, H), 1.0),
        "s5_glu_w": nrm(15, (N_EVEN, G, H, H), H ** -0.5),
        "s5_glu_b": nrm(16, (N_EVEN, G, H), 0.01),
        "q_norm_g": 1.0 + nrm(17, (N_EVEN, HEAD_DIM), 0.01),
        "k_norm_g": 1.0 + nrm(18, (N_EVEN, HEAD_DIM), 0.01),
        "odd_w_in": nrm(19, (N_ODD, D, 2 * SGU_WIDTH), D ** -0.5),
        "odd_w_out": nrm(20, (N_ODD, SGU_WIDTH, D), SGU_WIDTH ** -0.5),
        "sgu_ln_g": 1.0 + nrm(21, (N_ODD, SGU_GROUPS, SGU_GROUP_W), 0.01),
        "sgu_w": nrm(22, (N_ODD, SGU_GROUPS, SGU_CHUNK, SGU_CHUNK), 0.5 * SGU_CHUNK ** -0.5),
        "sgu_b": 1.0 + nrm(23, (N_ODD, SGU_GROUPS, SGU_CHUNK), 0.01),
        "peer_w_q": nrm(24, (DEPTH, D, PEER_HEADS * PEER_DKEY), D ** -0.5),
        "peer_k1": nrm(25, (DEPTH, PEER_NKEYS, PEER_DKEY // 2), (PEER_DKEY // 2) ** -0.5),
        "peer_k2": nrm(26, (DEPTH, PEER_NKEYS, PEER_DKEY // 2), (PEER_DKEY // 2) ** -0.5),
        "peer_u": nrm(27, (DEPTH, PEER_EXPERTS, D), D ** -0.5),
        "peer_v": nrm(28, (DEPTH, PEER_EXPERTS, D), 1.0),
        "final_g": 1.0 + nrm(29, (D,), 0.01),
    }


def reference(x, c, ada_w, ada_b, norm_g, even_w_in, even_w_out, s5_lam_re, s5_lam_im,
              s5_log_dt, s5_b_re, s5_b_im, s5_c_re, s5_c_im, s5_d, s5_glu_w, s5_glu_b,
              q_norm_g, k_norm_g, odd_w_in, odd_w_out, sgu_ln_g, sgu_w, sgu_b,
              peer_w_q, peer_k1, peer_k2, peer_u, peer_v, final_g):
    s_len = x.shape[1]
    cos, sin = axial_rope(s_len)
    c_act = jax.nn.silu(c)
    for layer in range(DEPTH):
        mod = c_act @ ada_w[layer] + ada_b[layer]
        sh1, sc1, g1, sh2, sc2, g2 = jnp.split(mod, 6, axis=-1)
        h = ada_rms_norm(x, norm_g[layer, 0], sh1, sc1)
        i = layer // 2
        if layer % 2 == 0:
            m = even_mixer(h, even_w_in[i], even_w_out[i], s5_lam_re[i], s5_lam_im[i],
                           s5_log_dt[i], s5_b_re[i], s5_b_im[i], s5_c_re[i], s5_c_im[i],
                           s5_d[i], s5_glu_w[i], s5_glu_b[i], q_norm_g[i], k_norm_g[i], cos, sin)
        else:
            m = odd_mixer(h, odd_w_in[i], odd_w_out[i], sgu_ln_g[i], sgu_w[i], sgu_b[i])
        x = x + g1[:, None, :] * m
        h = ada_rms_norm(x, norm_g[layer, 1], sh2, sc2)
        x = x + g2[:, None, :] * peer(h, peer_w_q[layer], peer_k1[layer], peer_k2[layer],
                                      peer_u[layer], peer_v[layer])
    return rms_norm(x, final_g)
```

```python
import functools
import math

import numpy as np
import jax
import jax.numpy as jnp
from jax import lax
from jax.experimental import pallas as pl
from jax.experimental.pallas import tpu as pltpu

F32 = jnp.float32
BF16 = jnp.bfloat16
EPS = 1e-6
HP = lax.Precision.HIGHEST

V7X_VMEM_LIMIT_BYTES = 56 * 1024 * 1024
LANES = 128
SUBLANES = 8

GRID_W = 64
S5_GROUP = 16
S5_STATE = 64
S5_CHUNK = 16
HEAD_DIM = 128
N_Q_HEADS = 8
N_KV_HEADS = 2
GQA_GROUP = N_Q_HEADS // N_KV_HEADS
ROPE_THETA = 10000.0
SGU_CHUNK = 128
SGU_GROUPS = 8
PEER_HEADS = 8
PEER_NKEYS = 128
PEER_TOPK = 16
DT_CLAMP = -1e-4


def _cparams(semantics):
    return pltpu.CompilerParams(dimension_semantics=semantics, vmem_limit_bytes=V7X_VMEM_LIMIT_BYTES)


def _gelu(x):
    return jax.nn.gelu(x)


def _mod_kernel(c_ref, w_ref, b_ref, o_ref):
    c = c_ref[...]
    ca = c * jax.nn.sigmoid(c)
    o_ref[0] = jnp.dot(ca, w_ref[0], preferred_element_type=F32, precision=HP) + b_ref[0]


def _modulation(c, ada_w, ada_b):
    depth, d, n = ada_w.shape
    bsz = c.shape[0]
    cpad = jnp.zeros((SUBLANES, d), F32).at[:bsz].set(c)
    tn = 1024
    out = pl.pallas_call(
        _mod_kernel,
        grid=(depth, n // tn),
        in_specs=[pl.BlockSpec((SUBLANES, d), lambda l, j: (0, 0)),
                  pl.BlockSpec((1, d, tn), lambda l, j: (l, 0, j)),
                  pl.BlockSpec((1, 1, tn), lambda l, j: (l, 0, j))],
        out_specs=pl.BlockSpec((1, SUBLANES, tn), lambda l, j: (l, 0, j)),
        out_shape=jax.ShapeDtypeStruct((depth, SUBLANES, n), F32),
        compiler_params=_cparams(("arbitrary", "arbitrary")),
        name="modulation",
    )(cpad, ada_w, ada_b.reshape(depth, 1, n))
    return out[:, :bsz]


def _ada_norm(x, g, sc, sh):
    ms = jnp.mean(x * x, axis=-1, keepdims=True)
    return x * lax.rsqrt(ms + EPS) * (g * (1.0 + sc)) + sh


def _front0_kernel(x_ref, g_ref, sc_ref, sh_ref, w_ref, gq_ref, gk_ref, cos_ref, sin_ref,
                   ua_ref, q_ref, k_ref, v_ref, *, n_a, n_q, n_kv):
    h = _ada_norm(x_ref[0], g_ref[...], sc_ref[0], sh_ref[0])
    z = jnp.dot(h.astype(BF16), w_ref[...], preferred_element_type=F32)
    ua_ref[0] = z[:, :n_a].astype(BF16)
    cos = cos_ref[...]
    sin = sin_ref[...]

    def norm_rope(y, g):
        y = y * lax.rsqrt(jnp.mean(y * y, axis=-1, keepdims=True) + EPS) * g
        return y * cos + pltpu.roll(y, HEAD_DIM // 2, 1) * sin

    for hd in range(n_q // HEAD_DIM):
        lo = n_a + hd * HEAD_DIM
        q_ref[0, :, hd * HEAD_DIM:(hd + 1) * HEAD_DIM] = norm_rope(z[:, lo:lo + HEAD_DIM], gq_ref[...]).astype(BF16)
    for hd in range(n_kv // HEAD_DIM):
        lo = n_a + n_q + hd * HEAD_DIM
        k_ref[0, :, hd * HEAD_DIM:(hd + 1) * HEAD_DIM] = norm_rope(z[:, lo:lo + HEAD_DIM], gk_ref[...]).astype(BF16)
    v_ref[0] = z[:, n_a + n_q + n_kv:].astype(BF16)


def _front0(x, g, sc, sh, w_in, gq, gk, cos_t, sin_t, n_a, n_q, n_kv, tm=256):
    bsz, s_len, d = x.shape
    n_all = w_in.shape[1]
    vec = lambda: pl.BlockSpec((1, 1, d), lambda b, i: (b, 0, 0))
    return pl.pallas_call(
        functools.partial(_front0_kernel, n_a=n_a, n_q=n_q, n_kv=n_kv),
        grid=(bsz, s_len // tm),
        in_specs=[pl.BlockSpec((1, tm, d), lambda b, i: (b, i, 0)),
                  pl.BlockSpec((1, d), lambda b, i: (0, 0)),
                  vec(), vec(),
                  pl.BlockSpec((d, n_all), lambda b, i: (0, 0)),
                  pl.BlockSpec((1, HEAD_DIM), lambda b, i: (0, 0)),
                  pl.BlockSpec((1, HEAD_DIM), lambda b, i: (0, 0)),
                  pl.BlockSpec((tm, HEAD_DIM), lambda b, i: (i, 0)),
                  pl.BlockSpec((tm, HEAD_DIM), lambda b, i: (i, 0))],
        out_specs=[pl.BlockSpec((1, tm, n_a), lambda b, i: (b, i, 0)),
                   pl.BlockSpec((1, tm, n_q), lambda b, i: (b, i, 0)),
                   pl.BlockSpec((1, tm, n_kv), lambda b, i: (b, i, 0)),
                   pl.BlockSpec((1, tm, n_kv), lambda b, i: (b, i, 0))],
        out_shape=[jax.ShapeDtypeStruct((bsz, s_len, n_a), BF16),
                   jax.ShapeDtypeStruct((bsz, s_len, n_q), BF16),
                   jax.ShapeDtypeStruct((bsz, s_len, n_kv), BF16),
                   jax.ShapeDtypeStruct((bsz, s_len, n_kv), BF16)],
        compiler_params=_cparams(("arbitrary", "arbitrary")),
        name="front0",
    )(x, g, sc, sh, w_in, gq, gk, cos_t, sin_t)


def _s5_weights(lam_re, lam_im, log_dt, b_re, b_im, c_re, c_im, d_skip, glu_w, glu_b, n_steps):
    L = S5_CHUNK
    n_g, n_p = lam_re.shape[1], lam_re.shape[2]
    n_h = b_re.shape[-1]
    lr = jnp.minimum(lam_re.astype(F32), DT_CLAMP)
    li = lam_im.astype(F32)
    dt = jnp.exp(log_dt.astype(F32))[..., None]
    mag = jnp.exp(lr * dt)
    ar = mag * jnp.cos(li * dt)
    ai = mag * jnp.sin(li * dt)
    den = lr * lr + li * li
    nr = ar - 1.0
    cr = (nr * lr + ai * li) / den
    ci = (ai * lr - nr * li) / den
    bbr = cr[..., None] * b_re - ci[..., None] * b_im
    bbi = cr[..., None] * b_im + ci[..., None] * b_re
    pr, pi = [jnp.ones_like(ar)], [jnp.zeros_like(ai)]
    for _ in range(L):
        pr_n = pr[-1] * ar - pi[-1] * ai
        pi_n = pr[-1] * ai + pi[-1] * ar
        pr.append(pr_n)
        pi.append(pi_n)
    pw_r = jnp.stack(pr)
    pw_i = jnp.stack(pi)
    cre = c_re.astype(F32)[None]
    cim = c_im.astype(F32)[None]
    cp_r = cre * pw_r[:, :, :, None, :] - cim * pw_i[:, :, :, None, :]
    cp_i = cre * pw_i[:, :, :, None, :] + cim * pw_r[:, :, :, None, :]
    kern = (jnp.einsum('ndghp,dgpk->ndghk', cp_r[:L], bbr, precision=HP)
            - jnp.einsum('ndghp,dgpk->ndghk', cp_i[:L], bbi, precision=HP))
    kf, kb = kern[:, 0], kern[:, 1]
    centre = kf[0] + kb[0] + d_skip.astype(F32)[:, :, None] * jnp.eye(n_h, dtype=F32)
    ktab = jnp.concatenate([kb[1:][::-1], centre[None], kf[1:]], axis=0)
    idx = np.arange(L)[:, None] - np.arange(L)[None, :] + L - 1
    m_full = ktab[idx]
    mt = m_full.transpose(2, 1, 4, 0, 3).reshape(n_g, L * n_h, L * n_h)

    def win_part(d, order):
        wr = pw_r[order, d][:, :, :, None] * bbr[d][None] - pw_i[order, d][:, :, :, None] * bbi[d][None]
        wi = pw_r[order, d][:, :, :, None] * bbi[d][None] + pw_i[order, d][:, :, :, None] * bbr[d][None]
        tr = lambda w: w.transpose(1, 0, 3, 2).reshape(n_g, L * n_h, n_p)
        return tr(wr), tr(wi)

    wf_r, wf_i = win_part(0, np.arange(L - 1, -1, -1))
    wb_r, wb_i = win_part(1, np.arange(L))
    win = jnp.concatenate([wf_r, wf_i, wb_r, wb_i], axis=-1)

    def wout_part(d, order):
        er = cp_r[order, d]
        ei = cp_i[order, d]
        tr = lambda w: w.transpose(1, 3, 0, 2).reshape(n_g, n_p, L * n_h)
        return tr(er), tr(-ei)

    of_r, of_i = wout_part(0, np.arange(1, L + 1))
    ob_r, ob_i = wout_part(1, np.arange(L, 0, -1))
    wout = jnp.concatenate([of_r, of_i, ob_r, ob_i], axis=1)

    glu = jnp.einsum('ts,ghk->gthsk', jnp.eye(L, dtype=F32), glu_w.astype(F32)).reshape(n_g, L * n_h, L * n_h)
    glub = jnp.tile(glu_b.astype(F32), (1, L)).reshape(n_g, 1, L * n_h)

    qr, qi = pw_r[L], pw_i[L]
    rows = []
    for _ in range(n_steps):
        rows.append((jnp.concatenate([qr, qr], -1), jnp.concatenate([-qi, qi], -1)))
        qr, qi = qr * qr - qi * qi, 2.0 * qr * qi
    coef = jnp.stack([jnp.stack([a1, a2], axis=0) for a1, a2 in rows], axis=0)
    coef = coef.transpose(3, 2, 0, 1, 4).reshape(n_g, 2 * n_steps * 2, 2 * n_p)
    return mt.astype(BF16), win.astype(BF16), wout.astype(BF16), glu.astype(BF16), glub, coef


def _s5_kernel(u_ref, mt_ref, win_ref, wout_ref, glu_ref, glub_ref, coef_ref, o_ref, *, bsz, n_steps):
    u = u_ref[0]
    rows_total = u.shape[0]
    n_c = rows_total // bsz
    st = jnp.dot(u, win_ref[0], preferred_element_type=F32)
    two_p = st.shape[1] // 2
    row = lax.broadcasted_iota(jnp.int32, (n_c, two_p), 0)

    def scan(x, d):
        for k in range(n_steps):
            sh = 1 << k
            if d == 0:
                shifted = jnp.where(row >= sh, pltpu.roll(x, sh, 0), 0.0)
            else:
                shifted = jnp.where(row < n_c - sh, pltpu.roll(x, n_c - sh, 0), 0.0)
            base = (d * n_steps + k) * 2
            a1 = coef_ref[0, base:base + 1, :]
            a2 = coef_ref[0, base + 1:base + 2, :]
            x = x + a1 * shifted + a2 * pltpu.roll(shifted, two_p // 2, 1)
        if d == 0:
            return jnp.where(row >= 1, pltpu.roll(x, 1, 0), 0.0)
        return jnp.where(row < n_c - 1, pltpu.roll(x, n_c - 1, 0), 0.0)

    parts = []
    for b in range(bsz):
        sb = st[b * n_c:(b + 1) * n_c]
        hf = scan(sb[:, :two_p], 0)
        hb = scan(sb[:, two_p:], 1)
        parts.append(jnp.concatenate([hf, hb], axis=1))
    hcat = jnp.concatenate(parts, axis=0).astype(BF16)
    y = (jnp.dot(u, mt_ref[0], preferred_element_type=F32)
         + jnp.dot(hcat, wout_ref[0], preferred_element_type=F32))
    y = _gelu(y)
    gate = jax.nn.sigmoid(jnp.dot(y.astype(BF16), glu_ref[0], preferred_element_type=F32) + glub_ref[0])
    o_ref[0] = (y * gate).astype(BF16)


def _s5(ua, weights, n_steps):
    mt, win, wout, glu, glub, coef = weights
    bsz, s_len, width = ua.shape
    n_g = mt.shape[0]
    n_h = width // n_g
    L = S5_CHUNK
    n_c = s_len // L
    ut = ua.reshape(bsz, n_c, L, n_g, n_h).transpose(3, 0, 1, 2, 4).reshape(n_g, bsz * n_c, L * n_h)
    lh = L * n_h
    blk = lambda a: pl.BlockSpec((1,) + a.shape[1:], lambda g: (g, 0, 0))
    out = pl.pallas_call(
        functools.partial(_s5_kernel, bsz=bsz, n_steps=n_steps),
        grid=(n_g,),
        in_specs=[blk(ut), blk(mt), blk(win), blk(wout), blk(glu), blk(glub), blk(coef)],
        out_specs=pl.BlockSpec((1, bsz * n_c, lh), lambda g: (g, 0, 0)),
        out_shape=jax.ShapeDtypeStruct((n_g, bsz * n_c, lh), BF16),
        compiler_params=_cparams(("arbitrary",)),
        name="s5",
    )(ut, mt, win, wout, glu, glub, coef)
    return out.reshape(n_g, bsz, n_c, L, n_h).transpose(1, 2, 3, 0, 4).reshape(bsz, s_len, width)


def _attn_kernel(q_ref, k_ref, v_ref, o_ref, *, tk):
    tq = q_ref.shape[1]
    qs = jnp.concatenate([q_ref[0, :, h * HEAD_DIM:(h + 1) * HEAD_DIM] for h in range(GQA_GROUP)], axis=0)
    n_rows = qs.shape[0]
    n_kv = k_ref.shape[1] // tk

    def body(j, carry):
        m, l, acc = carry
        start = pl.multiple_of(j * tk, tk)
        kc = k_ref[0, pl.ds(start, tk), :]
        vc = v_ref[0, pl.ds(start, tk), :]
        s = lax.dot_general(qs, kc, (((1,), (1,)), ((), ())), preferred_element_type=F32)
        m_new = jnp.maximum(m, jnp.max(s, axis=-1, keepdims=True))
        p = jnp.exp(s - m_new)
        alpha = jnp.exp(m - m_new)
        l = alpha * l + jnp.sum(p, axis=-1, keepdims=True)
        acc = alpha * acc + jnp.dot(p.astype(BF16), vc, preferred_element_type=F32)
        return m_new, l, acc

    m0 = jnp.full((n_rows, 1), -jnp.inf, F32)
    l0 = jnp.zeros((n_rows, 1), F32)
    a0 = jnp.zeros((n_rows, HEAD_DIM), F32)
    _, l, acc = lax.fori_loop(0, n_kv, body, (m0, l0, a0))
    out = acc / l
    for h in range(GQA_GROUP):
        o_ref[0, :, h * HEAD_DIM:(h + 1) * HEAD_DIM] = out[h * tq:(h + 1) * tq].astype(BF16)


def _attention(q, k, v, tq=128, tk=512):
    bsz, s_len, n_q = q.shape
    gw = GQA_GROUP * HEAD_DIM
    tk = min(tk, s_len)
    return pl.pallas_call(
        functools.partial(_attn_kernel, tk=tk),
        grid=(bsz, N_KV_HEADS, s_len // tq),
        in_specs=[pl.BlockSpec((1, tq, gw), lambda b, h, i: (b, i, h)),
                  pl.BlockSpec((1, s_len, HEAD_DIM), lambda b, h, i: (b, 0, h)),
                  pl.BlockSpec((1, s_len, HEAD_DIM), lambda b, h, i: (b, 0, h))],
        out_specs=pl.BlockSpec((1, tq, gw), lambda b, h, i: (b, i, h)),
        out_shape=jax.ShapeDtypeStruct((bsz, s_len, n_q), BF16),
        compiler_params=_cparams(("arbitrary", "arbitrary", "arbitrary")),
        name="attention",
    )(q, k, v)


def _mid0_kernel(a_ref, b_ref, wa_ref, wb_ref, x_ref, g1_ref, g_ref, sc_ref, sh_ref, x1_ref, ht_ref):
    m = (jnp.dot(a_ref[0], wa_ref[...], preferred_element_type=F32)
         + jnp.dot(b_ref[0], wb_ref[...], preferred_element_type=F32))
    x1 = x_ref[0] + g1_ref[0] * m
    x1_ref[0] = x1
    h2 = _ada_norm(x1, g_ref[...], sc_ref[0], sh_ref[0])
    ht_ref[...] = h2.T.astype(BF16)


def _mid0(a_out, b_out, w_a, w_b, x, g1, g, sc, sh, tm=256):
    bsz, s_len, d = x.shape
    n_t = s_len // tm
    wa_n, wb_n = w_a.shape[0], w_b.shape[0]
    vec = lambda: pl.BlockSpec((1, 1, d), lambda b, i: (b, 0, 0))
    return pl.pallas_call(
        _mid0_kernel,
        grid=(bsz, n_t),
        in_specs=[pl.BlockSpec((1, tm, wa_n), lambda b, i: (b, i, 0)),
                  pl.BlockSpec((1, tm, wb_n), lambda b, i: (b, i, 0)),
                  pl.BlockSpec((wa_n, d), lambda b, i: (0, 0)),
                  pl.BlockSpec((wb_n, d), lambda b, i: (0, 0)),
                  pl.BlockSpec((1, tm, d), lambda b, i: (b, i, 0)),
                  vec(),
                  pl.BlockSpec((1, d), lambda b, i: (0, 0)),
                  vec(), vec()],
        out_specs=[pl.BlockSpec((1, tm, d), lambda b, i: (b, i, 0)),
                   pl.BlockSpec((d, tm), lambda b, i: (0, b * n_t + i))],
        out_shape=[jax.ShapeDtypeStruct((bsz, s_len, d), F32),
                   jax.ShapeDtypeStruct((d, bsz * s_len), BF16)],
        compiler_params=_cparams(("arbitrary", "arbitrary")),
        name="mid0",
    )(a_out, b_out, w_a, w_b, x, g1, g, sc, sh)


def _odd_kernel(x_ref, g_ref, sc_ref, sh_ref, wu_ref, wv_ref, lng_ref, ws_ref, bs_ref, wo_ref,
                g1_ref, g2n_ref, sc2_ref, sh2_ref, x1_ref, ht_ref, h_scr, acc_scr):
    gi = pl.program_id(2)
    n_groups = pl.num_programs(2)

    @pl.when(gi == 0)
    def _():
        h_scr[...] = _ada_norm(x_ref[0], g_ref[...], sc_ref[0], sh_ref[0]).astype(BF16)
        acc_scr[...] = jnp.zeros_like(acc_scr)

    h = h_scr[...]
    zu = _gelu(jnp.dot(h, wu_ref[...], preferred_element_type=F32))
    zv = _gelu(jnp.dot(h, wv_ref[...], preferred_element_type=F32))
    mu = jnp.mean(zv, axis=-1, keepdims=True)
    cen = zv - mu
    var = jnp.mean(cen * cen, axis=-1, keepdims=True)
    vn = (cen * lax.rsqrt(var + EPS) * lng_ref[0]).astype(BF16)
    ws = ws_ref[0]
    bias = bs_ref[0]
    tm = zu.shape[0]
    gated = []
    for c in range(tm // SGU_CHUNK):
        sl = slice(c * SGU_CHUNK, (c + 1) * SGU_CHUNK)
        s = jnp.dot(ws, vn[sl], preferred_element_type=F32) + bias
        gated.append((zu[sl] * s).astype(BF16))
    gated = jnp.concatenate(gated, axis=0)
    acc_scr[...] += jnp.dot(gated, wo_ref[...], preferred_element_type=F32)

    @pl.when(gi == n_groups - 1)
    def _():
        x1 = x_ref[0] + g1_ref[0] * acc_scr[...]
        x1_ref[0] = x1
        h2 = _ada_norm(x1, g2n_ref[...], sc2_ref[0], sh2_ref[0])
        ht_ref[...] = h2.T.astype(BF16)


def _odd(x, g, sc, sh, w_in, ln_g, w_s, b_s, w_out, g1, g2n, sc2, sh2, tm=512):
    bsz, s_len, d = x.shape
    n_t = s_len // tm
    width = w_out.shape[0]
    gw = width // SGU_GROUPS
    vec = lambda: pl.BlockSpec((1, 1, d), lambda b, i, gi: (b, 0, 0))
    row = lambda: pl.BlockSpec((1, d), lambda b, i, gi: (0, 0))
    bias = jnp.broadcast_to(b_s.astype(F32)[:, :, None], (SGU_GROUPS, SGU_CHUNK, gw))
    return pl.pallas_call(
        _odd_kernel,
        grid=(bsz, n_t, SGU_GROUPS),
        in_specs=[pl.BlockSpec((1, tm, d), lambda b, i, gi: (b, i, 0)),
                  row(), vec(), vec(),
                  pl.BlockSpec((d, gw), lambda b, i, gi: (0, gi)),
                  pl.BlockSpec((d, gw), lambda b, i, gi: (0, SGU_GROUPS + gi)),
                  pl.BlockSpec((1, 1, gw), lambda b, i, gi: (gi, 0, 0)),
                  pl.BlockSpec((1, SGU_CHUNK, SGU_CHUNK), lambda b, i, gi: (gi, 0, 0)),
                  pl.BlockSpec((1, SGU_CHUNK, gw), lambda b, i, gi: (gi, 0, 0)),
                  pl.BlockSpec((gw, d), lambda b, i, gi: (gi, 0)),
                  vec(), row(), vec(), vec()],
        out_specs=[pl.BlockSpec((1, tm, d), lambda b, i, gi: (b, i, 0)),
                   pl.BlockSpec((d, tm), lambda b, i, gi: (0, b * n_t + i))],
        out_shape=[jax.ShapeDtypeStruct((bsz, s_len, d), F32),
                   jax.ShapeDtypeStruct((d, bsz * s_len), BF16)],
        scratch_shapes=[pltpu.VMEM((tm, d), BF16), pltpu.VMEM((tm, d), F32)],
        compiler_params=_cparams(("arbitrary", "arbitrary", "arbitrary")),
        name="odd_mixer",
    )(x, g, sc, sh, w_in, w_in, ln_g.reshape(SGU_GROUPS, 1, gw), w_s, bias, w_out, g1, g2n, sc2, sh2)


def _compose_kernel(k_ref, w_ref, o_ref):
    o_ref[0, 0] = lax.dot_general(k_ref[0], w_ref[...], (((1,), (1,)), ((), ())),
                                  preferred_element_type=F32, precision=HP)


def _compose_score_weights(w_q, k1, k2):
    d = w_q.shape[0]
    dk = k1.shape[1]
    keys = jnp.stack([k1, k2]).astype(F32)
    out = pl.pallas_call(
        _compose_kernel,
        grid=(2, PEER_HEADS),
        in_specs=[pl.BlockSpec((1, PEER_NKEYS, dk), lambda a, h: (a, 0, 0)),
                  pl.BlockSpec((d, dk), lambda a, h: (0, h * 2 + a))],
        out_specs=pl.BlockSpec((1, 1, PEER_NKEYS, d), lambda a, h: (a, h, 0, 0)),
        out_shape=jax.ShapeDtypeStruct((2, PEER_HEADS, PEER_NKEYS, d), F32),
        compiler_params=_cparams(("arbitrary", "arbitrary")),
        name="compose_scores",
    )(keys, w_q.astype(F32))
    return out.transpose(0, 2, 1, 3).reshape(2 * PEER_NKEYS * PEER_HEADS, d).astype(BF16)


def _batcher_pairs(n):
    pairs = []
    p = 1
    while p < n:
        k = p
        while k >= 1:
            for j in range(k % p, n - k, 2 * k):
                for i in range(min(k, n - j - k)):
                    if (i + j) // (2 * p) == (i + j + k) // (2 * p):
                        pairs.append((i + j, i + j + k))
            k //= 2
        p *= 2
    return pairs


_SORT16 = _batcher_pairs(PEER_TOPK)


def _vmax(a, b):
    if a is None:
        return b
    if b is None:
        return a
    return jnp.maximum(a, b)


def _vmin(a, b):
    if a is None or b is None:
        return None
    return jnp.minimum(a, b)


def _sort16_desc(v):
    v = list(v)
    for i, j in _SORT16:
        v[i], v[j] = _vmax(v[i], v[j]), _vmin(v[i], v[j])
    return v


def _merge_top16(a, b):
    n = PEER_TOPK
    c = [_vmax(a[i], b[n - 1 - i]) for i in range(n)]
    d = n // 2
    while d >= 1:
        for i in range(n):
            if (i & d) == 0:
                c[i], c[i + d] = _vmax(c[i], c[i + d]), _vmin(c[i], c[i + d])
        d //= 2
    return c


def _top16_sorted(vals):
    groups = [_sort16_desc(vals[i:i + PEER_TOPK]) for i in range(0, len(vals), PEER_TOPK)]
    while len(groups) > 1:
        groups = [_merge_top16(groups[i], groups[i + 1]) for i in range(0, len(groups), 2)]
    return groups[0]


def _route_kernel(ht_ref, w_ref, s1_ref, p1_ref, th_ref, s2_ref, p2_ref, s2_scr, p2_scr):
    nh, nk = PEER_HEADS, PEER_NKEYS
    st = jnp.dot(w_ref[...], ht_ref[...], preferred_element_type=F32)
    s1 = [st[i * nh:(i + 1) * nh] for i in range(nk)]
    s2 = [st[(nk + i) * nh:(nk + i + 1) * nh] for i in range(nk)]
    a = _top16_sorted(s1)
    b = _top16_sorted(s2)
    cands = [[a[k] + b[l] for l in range(PEER_TOPK // (k + 1))] for k in range(PEER_TOPK)]
    flat = [c for rowc in cands[1:] for c in rowc]
    flat = flat + [None] * (-len(flat) % PEER_TOPK)
    groups = [cands[0]] + [_sort16_desc(flat[i:i + PEER_TOPK]) for i in range(0, len(flat), PEER_TOPK)]
    top = groups[0]
    for gr in groups[1:]:
        top = _merge_top16(top, gr)
    theta = top[PEER_TOPK - 1]
    z = jnp.ones_like(theta)
    for t in top[1:]:
        z = z + jnp.exp(t - top[0])
    inv_z = 1.0 / z
    th_ref[...] = theta
    for i in range(nk):
        rows = slice(i * nh, (i + 1) * nh)
        s1_ref[rows, :] = s1[i]
        p1_ref[rows, :] = jnp.exp(s1[i] - a[0]) * inv_z
        p2 = jnp.exp(s2[i] - b[0])
        for lb in range(s2_scr.shape[0]):
            lanes = slice(lb * LANES, (lb + 1) * LANES)
            s2_scr[lb, rows, :] = s2[i][:, lanes]
            p2_scr[lb, rows, :] = p2[:, lanes]
    for h in range(nh):
        for lb in range(s2_scr.shape[0]):
            lanes = slice(lb * LANES, (lb + 1) * LANES)
            s2_ref[h, :, lanes] = s2_scr[lb, pl.ds(h, nk, stride=nh), :]
            p2_ref[h, :, lanes] = p2_scr[lb, pl.ds(h, nk, stride=nh), :]


def _route(ht, w_s, tm=256):
    d, t = ht.shape
    nh, nk = PEER_HEADS, PEER_NKEYS
    rows = nh * nk
    return pl.pallas_call(
        _route_kernel,
        grid=(t // tm,),
        in_specs=[pl.BlockSpec((d, tm), lambda i: (0, i)),
                  pl.BlockSpec((2 * rows, d), lambda i: (0, 0))],
        out_specs=[pl.BlockSpec((rows, tm), lambda i: (0, i)),
                   pl.BlockSpec((rows, tm), lambda i: (0, i)),
                   pl.BlockSpec((nh, tm), lambda i: (0, i)),
                   pl.BlockSpec((nh, nk, tm), lambda i: (0, 0, i)),
                   pl.BlockSpec((nh, nk, tm), lambda i: (0, 0, i))],
        out_shape=[jax.ShapeDtypeStruct((rows, t), F32),
                   jax.ShapeDtypeStruct((rows, t), F32),
                   jax.ShapeDtypeStruct((nh, t), F32),
                   jax.ShapeDtypeStruct((nh, nk, t), F32),
                   jax.ShapeDtypeStruct((nh, nk, t), F32)],
        scratch_shapes=[pltpu.VMEM((tm // LANES, rows, LANES), F32), pltpu.VMEM((tm // LANES, rows, LANES), F32)],
        compiler_params=_cparams(("arbitrary",)),
        name="peer_route",
    )(ht, w_s)


def _peer_kernel(ht_ref, u_ref, vt_ref, s1_ref, p1_ref, th_ref, s2_ref, p2_ref, o_ref):
    ei = pl.program_id(1)
    nh, nk = PEER_HEADS, PEER_NKEYS
    te = u_ref.shape[0]

    @pl.when(ei == 0)
    def _():
        o_ref[...] = jnp.zeros_like(o_ref)

    act = jnp.dot(u_ref[...], ht_ref[...], preferred_element_type=F32)
    ga = _gelu(act)
    blocks = []
    for ii in range(te // nk):
        key_i = ei * (te // nk) + ii
        gate = None
        for h in range(nh):
            r = key_i * nh + h
            s1row = s1_ref[pl.ds(r, 1), :]
            p1row = p1_ref[pl.ds(r, 1), :]
            keep = (s2_ref[h] + s1row) >= th_ref[h:h + 1, :]
            term = jnp.where(keep, p2_ref[h], 0.0) * p1row
            gate = term if gate is None else gate + term
        blocks.append((ga[ii * nk:(ii + 1) * nk] * gate).astype(BF16))
    wt = jnp.concatenate(blocks, axis=0)
    o_ref[...] += jnp.dot(vt_ref[...], wt, preferred_element_type=F32)


def _peer(ht, u_tab, vt_tab, s1, p1, th, s2, p2, tm=512, te=512):
    d, t = ht.shape
    n_e = u_tab.shape[0]
    nh, nk = PEER_HEADS, PEER_NKEYS
    rows = nh * nk
    return pl.pallas_call(
        _peer_kernel,
        grid=(t // tm, n_e // te),
        in_specs=[pl.BlockSpec((d, tm), lambda i, e: (0, i)),
                  pl.BlockSpec((te, d), lambda i, e: (e, 0)),
                  pl.BlockSpec((d, te), lambda i, e: (0, e)),
                  pl.BlockSpec((rows, tm), lambda i, e: (0, i)),
                  pl.BlockSpec((rows, tm), lambda i, e: (0, i)),
                  pl.BlockSpec((nh, tm), lambda i, e: (0, i)),
                  pl.BlockSpec((nh, nk, tm), lambda i, e: (0, 0, i)),
                  pl.BlockSpec((nh, nk, tm), lambda i, e: (0, 0, i))],
        out_specs=pl.BlockSpec((d, tm), lambda i, e: (0, i)),
        out_shape=jax.ShapeDtypeStruct((d, t), F32),
        compiler_params=_cparams(("arbitrary", "arbitrary")),
        name="peer_dense",
    )(ht, u_tab, vt_tab, s1, p1, th, s2, p2)


def _resid_kernel(x_ref, pt_ref, g2_ref, fg_ref, o_ref, *, final_norm):
    x2 = x_ref[0] + g2_ref[0] * pt_ref[...].T
    if final_norm:
        ms = jnp.mean(x2 * x2, axis=-1, keepdims=True)
        x2 = x2 * lax.rsqrt(ms + EPS) * fg_ref[...]
    o_ref[0] = x2


def _resid(x1, peer_t, g2, final_g, final_norm, tm=256):
    bsz, s_len, d = x1.shape
    n_t = s_len // tm
    return pl.pallas_call(
        functools.partial(_resid_kernel, final_norm=final_norm),
        grid=(bsz, n_t),
        in_specs=[pl.BlockSpec((1, tm, d), lambda b, i: (b, i, 0)),
                  pl.BlockSpec((d, tm), lambda b, i: (0, b * n_t + i)),
                  pl.BlockSpec((1, 1, d), lambda b, i: (b, 0, 0)),
                  pl.BlockSpec((1, d), lambda b, i: (0, 0))],
        out_specs=pl.BlockSpec((1, tm, d), lambda b, i: (b, i, 0)),
        out_shape=jax.ShapeDtypeStruct((bsz, s_len, d), F32),
        compiler_params=_cparams(("arbitrary", "arbitrary")),
        name="peer_residual",
    )(x1, peer_t, g2, final_g)


def _peer_layer(x1, ht, g2, w_q, k1, k2, u_tab, v_tab, final_g, final_norm):
    w_s = _compose_score_weights(w_q, k1, k2)
    s1, p1, th, s2, p2 = _route(ht, w_s)
    peer_t = _peer(ht, u_tab.astype(BF16), v_tab.T.astype(BF16), s1, p1, th, s2, p2)
    return _resid(x1, peer_t, g2, final_g, final_norm)


def _rope_tables(s_len):
    rows = s_len // GRID_W
    row = jnp.repeat(jnp.arange(rows, dtype=F32), GRID_W)
    col = jnp.tile(jnp.arange(GRID_W, dtype=F32), rows)
    n_pair_axis = HEAD_DIM // 4
    inv_freq = ROPE_THETA ** (-jnp.arange(n_pair_axis, dtype=F32) / n_pair_axis)
    ang = jnp.concatenate([row[:, None] * inv_freq, col[:, None] * inv_freq], axis=-1)
    cos, sin = jnp.cos(ang), jnp.sin(ang)
    return jnp.concatenate([cos, cos], axis=-1), jnp.concatenate([-sin, sin], axis=-1)


def _deinterleave_perm():
    half = HEAD_DIM // 2
    return np.concatenate([2 * np.arange(half), 2 * np.arange(half) + 1])


def kernel(x, c, ada_w, ada_b, norm_g, even_w_in, even_w_out, s5_lam_re, s5_lam_im, s5_log_dt, s5_b_re, s5_b_im, s5_c_re, s5_c_im, s5_d, s5_glu_w, s5_glu_b, q_norm_g, k_norm_g, odd_w_in, odd_w_out, sgu_ln_g, sgu_w, sgu_b, peer_w_q, peer_k1, peer_k2, peer_u, peer_v, final_g):
    bsz, s_len, d = x.shape
    depth = ada_w.shape[0]
    mod = _modulation(c, ada_w, ada_b)
    fg = final_g.reshape(1, d)
    n_a = s5_lam_re.shape[2] * S5_GROUP
    n_q = N_Q_HEADS * HEAD_DIM
    n_kv = N_KV_HEADS * HEAD_DIM
    n_steps = max(1, int(math.log2(s_len // S5_CHUNK)))
    cos_t, sin_t = _rope_tables(s_len)
    perm = _deinterleave_perm()

    for layer in range(depth):
        sh1, sc1, g1, sh2, sc2, g2 = [m.reshape(bsz, 1, d) for m in jnp.split(mod[layer], 6, axis=-1)]
        ng1 = norm_g[layer, 0].reshape(1, d)
        ng2 = norm_g[layer, 1].reshape(1, d)
        i = layer // 2
        if layer % 2 == 0:
            w_in = even_w_in[i]
            col = np.arange(w_in.shape[1])
            for hd in range(N_Q_HEADS + N_KV_HEADS):
                lo = n_a + hd * HEAD_DIM
                col[lo:lo + HEAD_DIM] = lo + perm
            w_in_p = w_in[:, col].astype(BF16)
            gq = (q_norm_g[i][perm] * (HEAD_DIM ** -0.5)).reshape(1, HEAD_DIM)
            gk = k_norm_g[i][perm].reshape(1, HEAD_DIM)
            ua, q, k, v = _front0(x, ng1, sc1, sh1, w_in_p, gq, gk, cos_t, sin_t, n_a, n_q, n_kv)
            weights = _s5_weights(s5_lam_re[i], s5_lam_im[i], s5_log_dt[i], s5_b_re[i], s5_b_im[i],
                                  s5_c_re[i], s5_c_im[i], s5_d[i], s5_glu_w[i], s5_glu_b[i], n_steps)
            a_out = _s5(ua, weights, n_steps)
            b_out = _attention(q, k, v)
            w_out = even_w_out[i].astype(BF16)
            x1, ht = _mid0(a_out, b_out, w_out[:n_a], w_out[n_a:], x, g1, ng2, sc2, sh2)
        else:
            x1, ht = _odd(x, ng1, sc1, sh1, odd_w_in[i].astype(BF16), sgu_ln_g[i], sgu_w[i].astype(BF16),
                          sgu_b[i], odd_w_out[i].astype(BF16), g1, ng2, sc2, sh2)
        x = _peer_layer(x1, ht, g2, peer_w_q[layer], peer_k1[layer], peer_k2[layer],
                        peer_u[layer], peer_v[layer], fg, layer == depth - 1)
    return x
```

```python
import functools
import math

import numpy as np
import jax
import jax.numpy as jnp
from jax import lax
from jax.experimental import pallas as pl
from jax.experimental.pallas import tpu as pltpu

F32 = jnp.float32
BF16 = jnp.bfloat16
EPS = 1e-6
HP = lax.Precision.HIGHEST

V7X_VMEM_LIMIT_BYTES = 56 * 1024 * 1024
LANES = 128
SUBLANES = 8

GRID_W = 64
S5_GROUP = 16
S5_STATE = 64
S5_CHUNK = 16
HEAD_DIM = 128
N_Q_HEADS = 8
N_KV_HEADS = 2
GQA_GROUP = N_Q_HEADS // N_KV_HEADS
ROPE_THETA = 10000.0
SGU_CHUNK = 128
SGU_GROUPS = 8
PEER_HEADS = 8
PEER_NKEYS = 128
PEER_TOPK = 16
DT_CLAMP = -1e-4


def _cparams(semantics):
    return pltpu.CompilerParams(dimension_semantics=semantics, vmem_limit_bytes=V7X_VMEM_LIMIT_BYTES)


def _gelu(x):
    return jax.nn.gelu(x)


def _mod_kernel(c_ref, w_ref, b_ref, o_ref):
    c = c_ref[...]
    ca = c * jax.nn.sigmoid(c)
    o_ref[0] = jnp.dot(ca, w_ref[0], preferred_element_type=F32, precision=HP) + b_ref[0]


def _modulation(c, ada_w, ada_b):
    depth, d, n = ada_w.shape
    bsz = c.shape[0]
    cpad = jnp.zeros((SUBLANES, d), F32).at[:bsz].set(c)
    tn = 1024
    out = pl.pallas_call(
        _mod_kernel,
        grid=(depth, n // tn),
        in_specs=[pl.BlockSpec((SUBLANES, d), lambda l, j: (0, 0)),
                  pl.BlockSpec((1, d, tn), lambda l, j: (l, 0, j)),
                  pl.BlockSpec((1, 1, tn), lambda l, j: (l, 0, j))],
        out_specs=pl.BlockSpec((1, SUBLANES, tn), lambda l, j: (l, 0, j)),
        out_shape=jax.ShapeDtypeStruct((depth, SUBLANES, n), F32),
        compiler_params=_cparams(("arbitrary", "arbitrary")),
        name="modulation",
    )(cpad, ada_w, ada_b.reshape(depth, 1, n))
    return out[:, :bsz]


def _ada_norm(x, g, sc, sh):
    ms = jnp.mean(x * x, axis=-1, keepdims=True)
    return x * lax.rsqrt(ms + EPS) * (g * (1.0 + sc)) + sh


def _front0_kernel(x_ref, g_ref, sc_ref, sh_ref, w_ref, gq_ref, gk_ref, cos_ref, sin_ref,
                   ua_ref, q_ref, k_ref, v_ref, *, n_a, n_q, n_kv):
    h = _ada_norm(x_ref[0], g_ref[...], sc_ref[0], sh_ref[0])
    z = jnp.dot(h.astype(BF16), w_ref[...], preferred_element_type=F32)
    ua_ref[0] = z[:, :n_a]
    cos = cos_ref[...]
    sin = sin_ref[...]

    def norm_rope(y, g):
        y = y * lax.rsqrt(jnp.mean(y * y, axis=-1, keepdims=True) + EPS) * g
        return y * cos + pltpu.roll(y, HEAD_DIM // 2, 1) * sin

    for hd in range(n_q // HEAD_DIM):
        lo = n_a + hd * HEAD_DIM
        q_ref[0, :, hd * HEAD_DIM:(hd + 1) * HEAD_DIM] = norm_rope(z[:, lo:lo + HEAD_DIM], gq_ref[...]).astype(BF16)
    for hd in range(n_kv // HEAD_DIM):
        lo = n_a + n_q + hd * HEAD_DIM
        k_ref[0, :, hd * HEAD_DIM:(hd + 1) * HEAD_DIM] = norm_rope(z[:, lo:lo + HEAD_DIM], gk_ref[...]).astype(BF16)
    v_ref[0] = z[:, n_a + n_q + n_kv:].astype(BF16)


def _front0(x, g, sc, sh, w_in, gq, gk, cos_t, sin_t, n_a, n_q, n_kv, tm=256):
    bsz, s_len, d = x.shape
    n_all = w_in.shape[1]
    vec = lambda: pl.BlockSpec((1, 1, d), lambda b, i: (b, 0, 0))
    return pl.pallas_call(
        functools.partial(_front0_kernel, n_a=n_a, n_q=n_q, n_kv=n_kv),
        grid=(bsz, s_len // tm),
        in_specs=[pl.BlockSpec((1, tm, d), lambda b, i: (b, i, 0)),
                  pl.BlockSpec((1, d), lambda b, i: (0, 0)),
                  vec(), vec(),
                  pl.BlockSpec((d, n_all), lambda b, i: (0, 0)),
                  pl.BlockSpec((1, HEAD_DIM), lambda b, i: (0, 0)),
                  pl.BlockSpec((1, HEAD_DIM), lambda b, i: (0, 0)),
                  pl.BlockSpec((tm, HEAD_DIM), lambda b, i: (i, 0)),
                  pl.BlockSpec((tm, HEAD_DIM), lambda b, i: (i, 0))],
        out_specs=[pl.BlockSpec((1, tm, n_a), lambda b, i: (b, i, 0)),
                   pl.BlockSpec((1, tm, n_q), lambda b, i: (b, i, 0)),
                   pl.BlockSpec((1, tm, n_kv), lambda b, i: (b, i, 0)),
                   pl.BlockSpec((1, tm, n_kv), lambda b, i: (b, i, 0))],
        out_shape=[jax.ShapeDtypeStruct((bsz, s_len, n_a), F32),
                   jax.ShapeDtypeStruct((bsz, s_len, n_q), BF16),
                   jax.ShapeDtypeStruct((bsz, s_len, n_kv), BF16),
                   jax.ShapeDtypeStruct((bsz, s_len, n_kv), BF16)],
        compiler_params=_cparams(("arbitrary", "arbitrary")),
        name="front0",
    )(x, g, sc, sh, w_in, gq, gk, cos_t, sin_t)


def _taps_kernel(cr_ref, ci_ref, br_ref, bi_ref, o_ref):
    for j in range(cr_ref.shape[0]):
        o_ref[j] = (jnp.dot(cr_ref[j], br_ref[j], preferred_element_type=F32, precision=HP)
                    - jnp.dot(ci_ref[j], bi_ref[j], preferred_element_type=F32, precision=HP))


def _s5_taps(cp_r, cp_i, bb_r, bb_i, per_step=8):
    n_b, rows, n_p = cp_r.shape
    n_h = bb_r.shape[-1]
    per_step = min(per_step, n_b)
    lhs = pl.BlockSpec((per_step, rows, n_p), lambda i: (i, 0, 0))
    rhs = pl.BlockSpec((per_step, n_p, n_h), lambda i: (i, 0, 0))
    return pl.pallas_call(
        _taps_kernel,
        grid=(n_b // per_step,),
        in_specs=[lhs, lhs, rhs, rhs],
        out_specs=pl.BlockSpec((per_step, rows, n_h), lambda i: (i, 0, 0)),
        out_shape=jax.ShapeDtypeStruct((n_b, rows, n_h), F32),
        compiler_params=_cparams(("arbitrary",)),
        name="s5_taps",
    )(cp_r, cp_i, bb_r, bb_i)


def _s5_weights(lam_re, lam_im, log_dt, b_re, b_im, c_re, c_im, d_skip, glu_w, glu_b, n_steps):
    L = S5_CHUNK
    n_g, n_p = lam_re.shape[1], lam_re.shape[2]
    n_h = b_re.shape[-1]
    lr = jnp.minimum(lam_re.astype(F32), DT_CLAMP)
    li = lam_im.astype(F32)
    dt = jnp.exp(log_dt.astype(F32))[..., None]
    mag = jnp.exp(lr * dt)
    ar = mag * jnp.cos(li * dt)
    ai = mag * jnp.sin(li * dt)
    den = lr * lr + li * li
    nr = ar - 1.0
    cr = (nr * lr + ai * li) / den
    ci = (ai * lr - nr * li) / den
    bbr = cr[..., None] * b_re - ci[..., None] * b_im
    bbi = cr[..., None] * b_im + ci[..., None] * b_re
    pr, pi = [jnp.ones_like(ar)], [jnp.zeros_like(ai)]
    for _ in range(L):
        pr_n = pr[-1] * ar - pi[-1] * ai
        pi_n = pr[-1] * ai + pi[-1] * ar
        pr.append(pr_n)
        pi.append(pi_n)
    pw_r = jnp.stack(pr)
    pw_i = jnp.stack(pi)
    cre = c_re.astype(F32)[None]
    cim = c_im.astype(F32)[None]
    cp_r = cre * pw_r[:, :, :, None, :] - cim * pw_i[:, :, :, None, :]
    cp_i = cre * pw_i[:, :, :, None, :] + cim * pw_r[:, :, :, None, :]
    to_rows = lambda c: c[:L].transpose(1, 2, 0, 3, 4).reshape(2 * n_g, L * n_h, n_p)
    taps = _s5_taps(to_rows(cp_r), to_rows(cp_i), bbr.reshape(2 * n_g, n_p, n_h), bbi.reshape(2 * n_g, n_p, n_h))
    kern = taps.reshape(2, n_g, L, n_h, n_h).transpose(2, 0, 1, 3, 4)
    kf, kb = kern[:, 0], kern[:, 1]
    centre = kf[0] + kb[0] + d_skip.astype(F32)[:, :, None] * jnp.eye(n_h, dtype=F32)
    ktab = jnp.concatenate([kb[1:][::-1], centre[None], kf[1:]], axis=0)
    idx = np.arange(L)[:, None] - np.arange(L)[None, :] + L - 1
    m_full = ktab[idx]
    mt = m_full.transpose(2, 1, 4, 0, 3).reshape(n_g, L * n_h, L * n_h)

    def win_part(d, order):
        wr = pw_r[order, d][:, :, :, None] * bbr[d][None] - pw_i[order, d][:, :, :, None] * bbi[d][None]
        wi = pw_r[order, d][:, :, :, None] * bbi[d][None] + pw_i[order, d][:, :, :, None] * bbr[d][None]
        tr = lambda w: w.transpose(1, 0, 3, 2).reshape(n_g, L * n_h, n_p)
        return tr(wr), tr(wi)

    wf_r, wf_i = win_part(0, np.arange(L - 1, -1, -1))
    wb_r, wb_i = win_part(1, np.arange(L))
    win = jnp.concatenate([wf_r, wf_i, wb_r, wb_i], axis=-1)

    def wout_part(d, order):
        er = cp_r[order, d]
        ei = cp_i[order, d]
        tr = lambda w: w.transpose(1, 3, 0, 2).reshape(n_g, n_p, L * n_h)
        return tr(er), tr(-ei)

    of_r, of_i = wout_part(0, np.arange(1, L + 1))
    ob_r, ob_i = wout_part(1, np.arange(L, 0, -1))
    wout = jnp.concatenate([of_r, of_i, ob_r, ob_i], axis=1)

    glu = jnp.einsum('ts,ghk->gthsk', jnp.eye(L, dtype=F32), glu_w.astype(F32)).reshape(n_g, L * n_h, L * n_h)
    glub = jnp.tile(glu_b.astype(F32), (1, L)).reshape(n_g, 1, L * n_h)

    qr, qi = pw_r[L], pw_i[L]
    rows = []
    for _ in range(n_steps):
        rows.append((jnp.concatenate([qr, qr], -1), jnp.concatenate([-qi, qi], -1)))
        qr, qi = qr * qr - qi * qi, 2.0 * qr * qi
    coef = jnp.stack([jnp.stack([a1, a2], axis=0) for a1, a2 in rows], axis=0)
    coef = coef.transpose(3, 2, 0, 1, 4).reshape(n_g, 2 * n_steps * 2, 2 * n_p)
    return mt.astype(BF16), win.astype(BF16), wout.astype(BF16), glu.astype(BF16), glub, coef


def _s5_kernel(u_ref, perm_ref, mt_ref, win_ref, wout_ref, glu_ref, glub_ref, coef_ref, o_ref, *, n_steps):
    L = S5_CHUNK
    n_c = u_ref.shape[1] // L
    gpb, lh = mt_ref.shape[0], mt_ref.shape[1]
    ww = jnp.concatenate([u_ref.at[0][pl.ds(t, n_c, stride=L), :].astype(BF16) for t in range(L)], axis=1)
    xp = jnp.dot(ww, perm_ref[...], preferred_element_type=F32).astype(BF16)
    row = lax.broadcasted_iota(jnp.int32, (n_c, 2 * S5_STATE), 0)

    def scan(x, g, d):
        for k in range(n_steps):
            sh = 1 << k
            if d == 0:
                shifted = jnp.where(row >= sh, pltpu.roll(x, sh, 0), 0.0)
            else:
                shifted = jnp.where(row < n_c - sh, pltpu.roll(x, n_c - sh, 0), 0.0)
            base = (d * n_steps + k) * 2
            a1 = coef_ref[g, base:base + 1, :]
            a2 = coef_ref[g, base + 1:base + 2, :]
            x = x + a1 * shifted + a2 * pltpu.roll(shifted, S5_STATE, 1)
        if d == 0:
            return jnp.where(row >= 1, pltpu.roll(x, 1, 0), 0.0)
        return jnp.where(row < n_c - 1, pltpu.roll(x, n_c - 1, 0), 0.0)

    outs = []
    for g in range(gpb):
        u = xp[:, g * lh:(g + 1) * lh]
        st = jnp.dot(u, win_ref[g], preferred_element_type=F32)
        two_p = st.shape[1] // 2
        hcat = jnp.concatenate([scan(st[:, :two_p], g, 0), scan(st[:, two_p:], g, 1)], axis=1).astype(BF16)
        y = (jnp.dot(u, mt_ref[g], preferred_element_type=F32)
             + jnp.dot(hcat, wout_ref[g], preferred_element_type=F32))
        y = _gelu(y)
        gate = jax.nn.sigmoid(jnp.dot(y.astype(BF16), glu_ref[g], preferred_element_type=F32) + glub_ref[g])
        outs.append((y * gate).astype(BF16))
    back = lax.dot_general(jnp.concatenate(outs, axis=1), perm_ref[...], (((1,), (1,)), ((), ())),
                           preferred_element_type=F32)
    for t in range(L):
        o_ref.at[0][pl.ds(t, n_c, stride=L), :] = back[:, t * LANES:(t + 1) * LANES]


def _s5(ua, weights, n_steps):
    mt, win, wout, glu, glub, coef = weights
    bsz, s_len, width = ua.shape
    n_g = mt.shape[0]
    n_h = width // n_g
    L = S5_CHUNK
    gpb = LANES // n_h
    n_lb = width // LANES
    cols = L * LANES
    ci = np.arange(cols)
    t_i, g_i, h_i = ci // LANES, (ci % LANES) // n_h, ci % n_h
    perm = (jnp.asarray(g_i * (L * n_h) + t_i * n_h + h_i)[:, None] == jnp.arange(cols)[None, :]).astype(BF16)
    blk = lambda a: pl.BlockSpec((gpb,) + a.shape[1:], lambda b, lb: (lb, 0, 0))
    return pl.pallas_call(
        functools.partial(_s5_kernel, n_steps=n_steps),
        grid=(bsz, n_lb),
        in_specs=[pl.BlockSpec((1, s_len, LANES), lambda b, lb: (b, 0, lb)),
                  pl.BlockSpec((cols, cols), lambda b, lb: (0, 0)),
                  blk(mt), blk(win), blk(wout), blk(glu), blk(glub), blk(coef)],
        out_specs=pl.BlockSpec((1, s_len, LANES), lambda b, lb: (b, 0, lb)),
        out_shape=jax.ShapeDtypeStruct((bsz, s_len, width), F32),
        compiler_params=_cparams(("arbitrary", "arbitrary")),
        name="s5",
    )(ua, perm, mt, win, wout, glu, glub, coef)


def _attn_kernel(q_ref, k_ref, v_ref, o_ref, *, tk):
    tq = q_ref.shape[1]
    qs = jnp.concatenate([q_ref[0, :, h * HEAD_DIM:(h + 1) * HEAD_DIM] for h in range(GQA_GROUP)], axis=0)
    n_rows = qs.shape[0]
    n_kv = k_ref.shape[1] // tk

    def body(j, carry):
        m, l, acc = carry
        start = pl.multiple_of(j * tk, tk)
        kc = k_ref[0, pl.ds(start, tk), :]
        vc = v_ref[0, pl.ds(start, tk), :]
        s = lax.dot_general(qs, kc, (((1,), (1,)), ((), ())), preferred_element_type=F32)
        m_new = jnp.maximum(m, jnp.max(s, axis=-1, keepdims=True))
        p = jnp.exp(s - m_new)
        alpha = jnp.exp(m - m_new)
        l = alpha * l + jnp.sum(p, axis=-1, keepdims=True)
        acc = alpha * acc + jnp.dot(p.astype(BF16), vc, preferred_element_type=F32)
        return m_new, l, acc

    m0 = jnp.full((n_rows, 1), -jnp.inf, F32)
    l0 = jnp.zeros((n_rows, 1), F32)
    a0 = jnp.zeros((n_rows, HEAD_DIM), F32)
    _, l, acc = lax.fori_loop(0, n_kv, body, (m0, l0, a0), unroll=True)
    out = acc / l
    for h in range(GQA_GROUP):
        o_ref[0, :, h * HEAD_DIM:(h + 1) * HEAD_DIM] = out[h * tq:(h + 1) * tq].astype(BF16)


def _attention(q, k, v, tq=128, tk=2048):
    bsz, s_len, n_q = q.shape
    gw = GQA_GROUP * HEAD_DIM
    tk = min(tk, s_len)
    return pl.pallas_call(
        functools.partial(_attn_kernel, tk=tk),
        grid=(bsz, N_KV_HEADS, s_len // tq),
        in_specs=[pl.BlockSpec((1, tq, gw), lambda b, h, i: (b, i, h)),
                  pl.BlockSpec((1, s_len, HEAD_DIM), lambda b, h, i: (b, 0, h)),
                  pl.BlockSpec((1, s_len, HEAD_DIM), lambda b, h, i: (b, 0, h))],
        out_specs=pl.BlockSpec((1, tq, gw), lambda b, h, i: (b, i, h)),
        out_shape=jax.ShapeDtypeStruct((bsz, s_len, n_q), BF16),
        compiler_params=_cparams(("arbitrary", "arbitrary", "arbitrary")),
        name="attention",
    )(q, k, v)


def _mid0_kernel(a_ref, b_ref, wa_ref, wb_ref, x_ref, g1_ref, g_ref, sc_ref, sh_ref, x1_ref, ht_ref):
    m = (jnp.dot(a_ref[0].astype(BF16), wa_ref[...], preferred_element_type=F32)
         + jnp.dot(b_ref[0], wb_ref[...], preferred_element_type=F32))
    x1 = x_ref[0] + g1_ref[0] * m
    x1_ref[0] = x1
    h2 = _ada_norm(x1, g_ref[...], sc_ref[0], sh_ref[0])
    ht_ref[...] = h2.T.astype(BF16)


def _mid0(a_out, b_out, w_a, w_b, x, g1, g, sc, sh, tm=256):
    bsz, s_len, d = x.shape
    n_t = s_len // tm
    wa_n, wb_n = w_a.shape[0], w_b.shape[0]
    vec = lambda: pl.BlockSpec((1, 1, d), lambda b, i: (b, 0, 0))
    return pl.pallas_call(
        _mid0_kernel,
        grid=(bsz, n_t),
        in_specs=[pl.BlockSpec((1, tm, wa_n), lambda b, i: (b, i, 0)),
                  pl.BlockSpec((1, tm, wb_n), lambda b, i: (b, i, 0)),
                  pl.BlockSpec((wa_n, d), lambda b, i: (0, 0)),
                  pl.BlockSpec((wb_n, d), lambda b, i: (0, 0)),
                  pl.BlockSpec((1, tm, d), lambda b, i: (b, i, 0)),
                  vec(),
                  pl.BlockSpec((1, d), lambda b, i: (0, 0)),
                  vec(), vec()],
        out_specs=[pl.BlockSpec((1, tm, d), lambda b, i: (b, i, 0)),
                   pl.BlockSpec((d, tm), lambda b, i: (0, b * n_t + i))],
        out_shape=[jax.ShapeDtypeStruct((bsz, s_len, d), F32),
                   jax.ShapeDtypeStruct((d, bsz * s_len), BF16)],
        compiler_params=_cparams(("arbitrary", "arbitrary")),
        name="mid0",
    )(a_out, b_out, w_a, w_b, x, g1, g, sc, sh)


def _odd_kernel(x_ref, g_ref, sc_ref, sh_ref, wu_ref, wv_ref, lng_ref, ws_ref, bs_ref, wo_ref,
                g1_ref, g2n_ref, sc2_ref, sh2_ref, x1_ref, ht_ref, h_scr, acc_scr):
    gi = pl.program_id(2)
    n_groups = pl.num_programs(2)

    @pl.when(gi == 0)
    def _():
        h_scr[...] = _ada_norm(x_ref[0], g_ref[...], sc_ref[0], sh_ref[0]).astype(BF16)
        acc_scr[...] = jnp.zeros_like(acc_scr)

    h = h_scr[...]
    zu = _gelu(jnp.dot(h, wu_ref[...], preferred_element_type=F32))
    zv = _gelu(jnp.dot(h, wv_ref[...], preferred_element_type=F32))
    mu = jnp.mean(zv, axis=-1, keepdims=True)
    cen = zv - mu
    var = jnp.mean(cen * cen, axis=-1, keepdims=True)
    vn = (cen * lax.rsqrt(var + EPS) * lng_ref[0]).astype(BF16)
    ws = ws_ref[0]
    bias = bs_ref[0]
    tm = zu.shape[0]
    gated = []
    for c in range(tm // SGU_CHUNK):
        sl = slice(c * SGU_CHUNK, (c + 1) * SGU_CHUNK)
        s = jnp.dot(ws, vn[sl], preferred_element_type=F32) + bias
        gated.append((zu[sl] * s).astype(BF16))
    gated = jnp.concatenate(gated, axis=0)
    acc_scr[...] += jnp.dot(gated, wo_ref[...], preferred_element_type=F32)

    @pl.when(gi == n_groups - 1)
    def _():
        x1 = x_ref[0] + g1_ref[0] * acc_scr[...]
        x1_ref[0] = x1
        h2 = _ada_norm(x1, g2n_ref[...], sc2_ref[0], sh2_ref[0])
        ht_ref[...] = h2.T.astype(BF16)


def _odd(x, g, sc, sh, w_in, ln_g, w_s, b_s, w_out, g1, g2n, sc2, sh2, tm=512):
    bsz, s_len, d = x.shape
    n_t = s_len // tm
    width = w_out.shape[0]
    gw = width // SGU_GROUPS
    vec = lambda: pl.BlockSpec((1, 1, d), lambda b, i, gi: (b, 0, 0))
    row = lambda: pl.BlockSpec((1, d), lambda b, i, gi: (0, 0))
    bias = jnp.broadcast_to(b_s.astype(F32)[:, :, None], (SGU_GROUPS, SGU_CHUNK, gw))
    return pl.pallas_call(
        _odd_kernel,
        grid=(bsz, n_t, SGU_GROUPS),
        in_specs=[pl.BlockSpec((1, tm, d), lambda b, i, gi: (b, i, 0)),
                  row(), vec(), vec(),
                  pl.BlockSpec((d, gw), lambda b, i, gi: (0, gi)),
                  pl.BlockSpec((d, gw), lambda b, i, gi: (0, SGU_GROUPS + gi)),
                  pl.BlockSpec((1, 1, gw), lambda b, i, gi: (gi, 0, 0)),
                  pl.BlockSpec((1, SGU_CHUNK, SGU_CHUNK), lambda b, i, gi: (gi, 0, 0)),
                  pl.BlockSpec((1, SGU_CHUNK, gw), lambda b, i, gi: (gi, 0, 0)),
                  pl.BlockSpec((gw, d), lambda b, i, gi: (gi, 0)),
                  vec(), row(), vec(), vec()],
        out_specs=[pl.BlockSpec((1, tm, d), lambda b, i, gi: (b, i, 0)),
                   pl.BlockSpec((d, tm), lambda b, i, gi: (0, b * n_t + i))],
        out_shape=[jax.ShapeDtypeStruct((bsz, s_len, d), F32),
                   jax.ShapeDtypeStruct((d, bsz * s_len), BF16)],
        scratch_shapes=[pltpu.VMEM((tm, d), BF16), pltpu.VMEM((tm, d), F32)],
        compiler_params=_cparams(("arbitrary", "arbitrary", "arbitrary")),
        name="odd_mixer",
    )(x, g, sc, sh, w_in, w_in, ln_g.reshape(SGU_GROUPS, 1, gw), w_s, bias, w_out, g1, g2n, sc2, sh2)


def _compose_kernel(k_ref, w_ref, o_ref):
    o_ref[0, 0] = lax.dot_general(k_ref[0], w_ref[...], (((1,), (1,)), ((), ())),
                                  preferred_element_type=F32, precision=HP)


def _compose_score_weights(w_q, k1, k2):
    d = w_q.shape[0]
    dk = k1.shape[1]
    keys = jnp.stack([k1, k2]).astype(F32)
    out = pl.pallas_call(
        _compose_kernel,
        grid=(2, PEER_HEADS),
        in_specs=[pl.BlockSpec((1, PEER_NKEYS, dk), lambda a, h: (a, 0, 0)),
                  pl.BlockSpec((d, dk), lambda a, h: (0, h * 2 + a))],
        out_specs=pl.BlockSpec((1, 1, PEER_NKEYS, d), lambda a, h: (a, h, 0, 0)),
        out_shape=jax.ShapeDtypeStruct((2, PEER_HEADS, PEER_NKEYS, d), F32),
        compiler_params=_cparams(("arbitrary", "arbitrary")),
        name="compose_scores",
    )(keys, w_q.astype(F32))
    return out.transpose(0, 2, 1, 3).reshape(2 * PEER_NKEYS * PEER_HEADS, d).astype(BF16)


def _batcher_pairs(n):
    pairs = []
    p = 1
    while p < n:
        k = p
        while k >= 1:
            for j in range(k % p, n - k, 2 * k):
                for i in range(min(k, n - j - k)):
                    if (i + j) // (2 * p) == (i + j + k) // (2 * p):
                        pairs.append((i + j, i + j + k))
            k //= 2
        p *= 2
    return pairs


_SORT16 = _batcher_pairs(PEER_TOPK)


def _vmax(a, b):
    if a is None:
        return b
    if b is None:
        return a
    return jnp.maximum(a, b)


def _vmin(a, b):
    if a is None or b is None:
        return None
    return jnp.minimum(a, b)


def _sort16_desc(v):
    v = list(v)
    for i, j in _SORT16:
        v[i], v[j] = _vmax(v[i], v[j]), _vmin(v[i], v[j])
    return v


def _merge_top16(a, b):
    n = PEER_TOPK
    c = [_vmax(a[i], b[n - 1 - i]) for i in range(n)]
    d = n // 2
    while d >= 1:
        for i in range(n):
            if (i & d) == 0:
                c[i], c[i + d] = _vmax(c[i], c[i + d]), _vmin(c[i], c[i + d])
        d //= 2
    return c


def _top16_sorted(vals):
    groups = [_sort16_desc(vals[i:i + PEER_TOPK]) for i in range(0, len(vals), PEER_TOPK)]
    while len(groups) > 1:
        groups = [_merge_top16(groups[i], groups[i + 1]) for i in range(0, len(groups), 2)]
    return groups[0]


def _route_kernel(ht_ref, w_ref, cnt_ref, p1_ref, r2_ref, p2_ref, r2_scr, p2_scr):
    nh, nk = PEER_HEADS, PEER_NKEYS
    st = jnp.dot(w_ref[...], ht_ref[...], preferred_element_type=F32)
    s1 = [st[i * nh:(i + 1) * nh] for i in range(nk)]
    s2 = [st[(nk + i) * nh:(nk + i + 1) * nh] for i in range(nk)]
    a = _top16_sorted(s1)
    b = _top16_sorted(s2)
    cands = [[a[k] + b[l] for l in range(PEER_TOPK // (k + 1))] for k in range(PEER_TOPK)]
    flat = [c for rowc in cands[1:] for c in rowc]
    flat = flat + [None] * (-len(flat) % PEER_TOPK)
    groups = [cands[0]] + [_sort16_desc(flat[i:i + PEER_TOPK]) for i in range(0, len(flat), PEER_TOPK)]
    top = groups[0]
    for gr in groups[1:]:
        top = _merge_top16(top, gr)
    theta = top[PEER_TOPK - 1]
    z = jnp.ones_like(theta)
    for t in top[1:]:
        z = z + jnp.exp(t - top[0])
    inv_z = 1.0 / z
    n_lb = r2_scr.shape[0]
    for i in range(nk):
        rows = slice(i * nh, (i + 1) * nh)
        cnt = jnp.zeros_like(theta)
        rank = jnp.zeros_like(theta)
        for l in range(PEER_TOPK):
            cnt = cnt + jnp.where(s1[i] + b[l] >= theta, 1.0, 0.0)
            rank = rank + jnp.where(b[l] > s2[i], 1.0, 0.0)
        cnt_ref[rows, :] = cnt
        p1_ref[rows, :] = jnp.exp(s1[i] - a[0]) * inv_z
        p2 = jnp.exp(s2[i] - b[0])
        for lb in range(n_lb):
            lanes = slice(lb * LANES, (lb + 1) * LANES)
            r2_scr[lb, rows, :] = rank[:, lanes]
            p2_scr[lb, rows, :] = p2[:, lanes]
    for h in range(nh):
        for lb in range(n_lb):
            lanes = slice(lb * LANES, (lb + 1) * LANES)
            r2_ref[h, :, lanes] = r2_scr.at[lb][pl.ds(h, nk, stride=nh), :].astype(BF16)
            p2_ref[h, :, lanes] = p2_scr.at[lb][pl.ds(h, nk, stride=nh), :].astype(BF16)


def _route(ht, w_s, tm=256):
    d, t = ht.shape
    nh, nk = PEER_HEADS, PEER_NKEYS
    rows = nh * nk
    return pl.pallas_call(
        _route_kernel,
        grid=(t // tm,),
        in_specs=[pl.BlockSpec((d, tm), lambda i: (0, i)),
                  pl.BlockSpec((2 * rows, d), lambda i: (0, 0))],
        out_specs=[pl.BlockSpec((rows, tm), lambda i: (0, i)),
                   pl.BlockSpec((rows, tm), lambda i: (0, i)),
                   pl.BlockSpec((nh, nk, tm), lambda i: (0, 0, i)),
                   pl.BlockSpec((nh, nk, tm), lambda i: (0, 0, i))],
        out_shape=[jax.ShapeDtypeStruct((rows, t), F32),
                   jax.ShapeDtypeStruct((rows, t), F32),
                   jax.ShapeDtypeStruct((nh, nk, t), BF16),
                   jax.ShapeDtypeStruct((nh, nk, t), BF16)],
        scratch_shapes=[pltpu.VMEM((tm // LANES, rows, LANES), F32), pltpu.VMEM((tm // LANES, rows, LANES), F32)],
        compiler_params=_cparams(("arbitrary",)),
        name="peer_route",
    )(ht, w_s)


def _peer_kernel(ht_ref, u_ref, vt_ref, cnt_ref, p1_ref, r2_ref, p2_ref, o_ref, *, n_split):
    ei = pl.program_id(1)
    nh, nk = PEER_HEADS, PEER_NKEYS
    te = u_ref.shape[0]
    ts = te // n_split

    @pl.when(ei == 0)
    def _():
        o_ref[...] = jnp.zeros_like(o_ref)

    gates = []
    for ii in range(te // nk):
        key_i = ei * (te // nk) + ii
        gate = None
        for h in range(nh):
            r = key_i * nh + h
            cnt = cnt_ref[pl.ds(r, 1), :].astype(BF16)
            p1 = p1_ref[pl.ds(r, 1), :].astype(BF16)
            p2 = p2_ref[h]
            term = jnp.where(r2_ref[h] < cnt, p2, jnp.zeros_like(p2)) * p1
            gate = term if gate is None else gate + term
        gates.append(gate)

    wts = []
    for sp in range(n_split):
        act = jnp.dot(u_ref[sp * ts:(sp + 1) * ts, :], ht_ref[...], preferred_element_type=F32)
        gate = jnp.concatenate(gates[sp * (ts // nk):(sp + 1) * (ts // nk)], axis=0)
        wts.append(_gelu(act).astype(BF16) * gate)
    wt = jnp.concatenate(wts, axis=0)
    o_ref[...] += jnp.dot(vt_ref[...], wt, preferred_element_type=F32)


def _peer(ht, u_tab, vt_tab, cnt, p1, r2, p2, tm=512, te=1024, n_split=8):
    d, t = ht.shape
    n_e = u_tab.shape[0]
    nh, nk = PEER_HEADS, PEER_NKEYS
    rows = nh * nk
    return pl.pallas_call(
        functools.partial(_peer_kernel, n_split=n_split),
        grid=(t // tm, n_e // te),
        in_specs=[pl.BlockSpec((d, tm), lambda i, e: (0, i)),
                  pl.BlockSpec((te, d), lambda i, e: (e, 0)),
                  pl.BlockSpec((d, te), lambda i, e: (0, e)),
                  pl.BlockSpec((rows, tm), lambda i, e: (0, i)),
                  pl.BlockSpec((rows, tm), lambda i, e: (0, i)),
                  pl.BlockSpec((nh, nk, tm), lambda i, e: (0, 0, i)),
                  pl.BlockSpec((nh, nk, tm), lambda i, e: (0, 0, i))],
        out_specs=pl.BlockSpec((d, tm), lambda i, e: (0, i)),
        out_shape=jax.ShapeDtypeStruct((d, t), F32),
        compiler_params=_cparams(("arbitrary", "arbitrary")),
        name="peer_dense",
    )(ht, u_tab, vt_tab, cnt, p1, r2, p2)


def _resid_kernel(x_ref, pt_ref, g2_ref, fg_ref, o_ref, *, final_norm):
    x2 = x_ref[0] + g2_ref[0] * pt_ref[...].T
    if final_norm:
        ms = jnp.mean(x2 * x2, axis=-1, keepdims=True)
        x2 = x2 * lax.rsqrt(ms + EPS) * fg_ref[...]
    o_ref[0] = x2


def _resid(x1, peer_t, g2, final_g, final_norm, tm=256):
    bsz, s_len, d = x1.shape
    n_t = s_len // tm
    return pl.pallas_call(
        functools.partial(_resid_kernel, final_norm=final_norm),
        grid=(bsz, n_t),
        in_specs=[pl.BlockSpec((1, tm, d), lambda b, i: (b, i, 0)),
                  pl.BlockSpec((d, tm), lambda b, i: (0, b * n_t + i)),
                  pl.BlockSpec((1, 1, d), lambda b, i: (b, 0, 0)),
                  pl.BlockSpec((1, d), lambda b, i: (0, 0))],
        out_specs=pl.BlockSpec((1, tm, d), lambda b, i: (b, i, 0)),
        out_shape=jax.ShapeDtypeStruct((bsz, s_len, d), F32),
        compiler_params=_cparams(("arbitrary", "arbitrary")),
        name="peer_residual",
    )(x1, peer_t, g2, final_g)


def _peer_layer(x1, ht, g2, w_q, k1, k2, u_tab, v_tab, final_g, final_norm):
    w_s = _compose_score_weights(w_q, k1, k2)
    cnt, p1, r2, p2 = _route(ht, w_s)
    peer_t = _peer(ht, u_tab.astype(BF16), v_tab.T.astype(BF16), cnt, p1, r2, p2)
    return _resid(x1, peer_t, g2, final_g, final_norm)


def _rope_tables(s_len):
    rows = s_len // GRID_W
    row = jnp.repeat(jnp.arange(rows, dtype=F32), GRID_W)
    col = jnp.tile(jnp.arange(GRID_W, dtype=F32), rows)
    n_pair_axis = HEAD_DIM // 4
    inv_freq = ROPE_THETA ** (-jnp.arange(n_pair_axis, dtype=F32) / n_pair_axis)
    ang = jnp.concatenate([row[:, None] * inv_freq, col[:, None] * inv_freq], axis=-1)
    cos, sin = jnp.cos(ang), jnp.sin(ang)
    return jnp.concatenate([cos, cos], axis=-1), jnp.concatenate([-sin, sin], axis=-1)


def _deinterleave_perm():
    half = HEAD_DIM // 2
    return np.concatenate([2 * np.arange(half), 2 * np.arange(half) + 1])


def kernel(x, c, ada_w, ada_b, norm_g, even_w_in, even_w_out, s5_lam_re, s5_lam_im, s5_log_dt, s5_b_re, s5_b_im, s5_c_re, s5_c_im, s5_d, s5_glu_w, s5_glu_b, q_norm_g, k_norm_g, odd_w_in, odd_w_out, sgu_ln_g, sgu_w, sgu_b, peer_w_q, peer_k1, peer_k2, peer_u, peer_v, final_g):
    bsz, s_len, d = x.shape
    depth = ada_w.shape[0]
    mod = _modulation(c, ada_w, ada_b)
    fg = final_g.reshape(1, d)
    n_a = s5_lam_re.shape[2] * S5_GROUP
    n_q = N_Q_HEADS * HEAD_DIM
    n_kv = N_KV_HEADS * HEAD_DIM
    n_steps = max(1, int(math.log2(s_len // S5_CHUNK)))
    cos_t, sin_t = _rope_tables(s_len)
    perm = _deinterleave_perm()

    for layer in range(depth):
        sh1, sc1, g1, sh2, sc2, g2 = [m.reshape(bsz, 1, d) for m in jnp.split(mod[layer], 6, axis=-1)]
        ng1 = norm_g[layer, 0].reshape(1, d)
        ng2 = norm_g[layer, 1].reshape(1, d)
        i = layer // 2
        if layer % 2 == 0:
            w_in = even_w_in[i]
            col = np.arange(w_in.shape[1])
            for hd in range(N_Q_HEADS + N_KV_HEADS):
                lo = n_a + hd * HEAD_DIM
                col[lo:lo + HEAD_DIM] = lo + perm
            w_in_p = w_in[:, col].astype(BF16)
            gq = (q_norm_g[i][perm] * (HEAD_DIM ** -0.5)).reshape(1, HEAD_DIM)
            gk = k_norm_g[i][perm].reshape(1, HEAD_DIM)
            ua, q, k, v = _front0(x, ng1, sc1, sh1, w_in_p, gq, gk, cos_t, sin_t, n_a, n_q, n_kv)
            weights = _s5_weights(s5_lam_re[i], s5_lam_im[i], s5_log_dt[i], s5_b_re[i], s5_b_im[i],
                                  s5_c_re[i], s5_c_im[i], s5_d[i], s5_glu_w[i], s5_glu_b[i], n_steps)
            a_out = _s5(ua, weights, n_steps)
            b_out = _attention(q, k, v)
            w_out = even_w_out[i].astype(BF16)
            x1, ht = _mid0(a_out, b_out, w_out[:n_a], w_out[n_a:], x, g1, ng2, sc2, sh2)
        else:
            x1, ht = _odd(x, ng1, sc1, sh1, odd_w_in[i].astype(BF16), sgu_ln_g[i], sgu_w[i].astype(BF16),
                          sgu_b[i], odd_w_out[i].astype(BF16), g1, ng2, sc2, sh2)
        x = _peer_layer(x1, ht, g2, peer_w_q[layer], peer_k1[layer], peer_k2[layer],
                        peer_u[layer], peer_v[layer], fg, layer == depth - 1)
    return x
```

```python
import functools
import math

import numpy as np
import jax
import jax.numpy as jnp
from jax import lax
from jax.experimental import pallas as pl
from jax.experimental.pallas import tpu as pltpu

F32 = jnp.float32
BF16 = jnp.bfloat16
EPS = 1e-6
HP = lax.Precision.HIGHEST

V7X_VMEM_LIMIT_BYTES = 56 * 1024 * 1024
LANES = 128
SUBLANES = 8

GRID_W = 64
S5_GROUP = 16
S5_STATE = 64
S5_CHUNK = 16
HEAD_DIM = 128
N_Q_HEADS = 8
N_KV_HEADS = 2
GQA_GROUP = N_Q_HEADS // N_KV_HEADS
ROPE_THETA = 10000.0
SGU_CHUNK = 128
SGU_GROUPS = 8
PEER_HEADS = 8
PEER_NKEYS = 128
PEER_TOPK = 16
DT_CLAMP = -1e-4


def _cparams(semantics):
    return pltpu.CompilerParams(dimension_semantics=semantics, vmem_limit_bytes=V7X_VMEM_LIMIT_BYTES)


def _gelu(x):
    return jax.nn.gelu(x)


def _mod_kernel(c_ref, w_ref, b_ref, o_ref):
    c = c_ref[...]
    ca = c * jax.nn.sigmoid(c)
    o_ref[0] = jnp.dot(ca, w_ref[0], preferred_element_type=F32, precision=HP) + b_ref[0]


def _modulation(c, ada_w, ada_b):
    depth, d, n = ada_w.shape
    bsz = c.shape[0]
    cpad = jnp.zeros((SUBLANES, d), F32).at[:bsz].set(c)
    tn = 1024
    out = pl.pallas_call(
        _mod_kernel,
        grid=(depth, n // tn),
        in_specs=[pl.BlockSpec((SUBLANES, d), lambda l, j: (0, 0)),
                  pl.BlockSpec((1, d, tn), lambda l, j: (l, 0, j)),
                  pl.BlockSpec((1, 1, tn), lambda l, j: (l, 0, j))],
        out_specs=pl.BlockSpec((1, SUBLANES, tn), lambda l, j: (l, 0, j)),
        out_shape=jax.ShapeDtypeStruct((depth, SUBLANES, n), F32),
        compiler_params=_cparams(("arbitrary", "arbitrary")),
        name="modulation",
    )(cpad, ada_w, ada_b.reshape(depth, 1, n))
    return out[:, :bsz]


def _ada_norm(x, g, sc, sh):
    ms = jnp.mean(x * x, axis=-1, keepdims=True)
    return x * lax.rsqrt(ms + EPS) * (g * (1.0 + sc)) + sh


def _front0_kernel(x_ref, g_ref, sc_ref, sh_ref, w_ref, gq_ref, gk_ref, cos_ref, sin_ref,
                   ua_ref, q_ref, k_ref, v_ref, *, n_a, n_q, n_kv):
    h = _ada_norm(x_ref[0], g_ref[...], sc_ref[0], sh_ref[0])
    z = jnp.dot(h.astype(BF16), w_ref[...], preferred_element_type=F32)
    ua_ref[0] = z[:, :n_a]
    cos = cos_ref[...]
    sin = sin_ref[...]

    def norm_rope(y, g):
        y = y * lax.rsqrt(jnp.mean(y * y, axis=-1, keepdims=True) + EPS) * g
        return y * cos + pltpu.roll(y, HEAD_DIM // 2, 1) * sin

    for hd in range(n_q // HEAD_DIM):
        lo = n_a + hd * HEAD_DIM
        q_ref[0, :, hd * HEAD_DIM:(hd + 1) * HEAD_DIM] = norm_rope(z[:, lo:lo + HEAD_DIM], gq_ref[...]).astype(BF16)
    for hd in range(n_kv // HEAD_DIM):
        lo = n_a + n_q + hd * HEAD_DIM
        k_ref[0, :, hd * HEAD_DIM:(hd + 1) * HEAD_DIM] = norm_rope(z[:, lo:lo + HEAD_DIM], gk_ref[...]).astype(BF16)
    v_ref[0] = z[:, n_a + n_q + n_kv:].astype(BF16)


def _front0(x, g, sc, sh, w_in, gq, gk, cos_t, sin_t, n_a, n_q, n_kv, tm=256):
    bsz, s_len, d = x.shape
    n_all = w_in.shape[1]
    vec = lambda: pl.BlockSpec((1, 1, d), lambda b, i: (b, 0, 0))
    return pl.pallas_call(
        functools.partial(_front0_kernel, n_a=n_a, n_q=n_q, n_kv=n_kv),
        grid=(bsz, s_len // tm),
        in_specs=[pl.BlockSpec((1, tm, d), lambda b, i: (b, i, 0)),
                  pl.BlockSpec((1, d), lambda b, i: (0, 0)),
                  vec(), vec(),
                  pl.BlockSpec((d, n_all), lambda b, i: (0, 0)),
                  pl.BlockSpec((1, HEAD_DIM), lambda b, i: (0, 0)),
                  pl.BlockSpec((1, HEAD_DIM), lambda b, i: (0, 0)),
                  pl.BlockSpec((tm, HEAD_DIM), lambda b, i: (i, 0)),
                  pl.BlockSpec((tm, HEAD_DIM), lambda b, i: (i, 0))],
        out_specs=[pl.BlockSpec((1, tm, n_a), lambda b, i: (b, i, 0)),
                   pl.BlockSpec((1, tm, n_q), lambda b, i: (b, i, 0)),
                   pl.BlockSpec((1, tm, n_kv), lambda b, i: (b, i, 0)),
                   pl.BlockSpec((1, tm, n_kv), lambda b, i: (b, i, 0))],
        out_shape=[jax.ShapeDtypeStruct((bsz, s_len, n_a), F32),
                   jax.ShapeDtypeStruct((bsz, s_len, n_q), BF16),
                   jax.ShapeDtypeStruct((bsz, s_len, n_kv), BF16),
                   jax.ShapeDtypeStruct((bsz, s_len, n_kv), BF16)],
        compiler_params=_cparams(("arbitrary", "arbitrary")),
        name="front0",
    )(x, g, sc, sh, w_in, gq, gk, cos_t, sin_t)


def _taps_kernel(cr_ref, ci_ref, br_ref, bi_ref, o_ref):
    for j in range(cr_ref.shape[0]):
        o_ref[j] = (jnp.dot(cr_ref[j], br_ref[j], preferred_element_type=F32, precision=HP)
                    - jnp.dot(ci_ref[j], bi_ref[j], preferred_element_type=F32, precision=HP))


def _s5_taps(cp_r, cp_i, bb_r, bb_i, per_step=8):
    n_b, rows, n_p = cp_r.shape
    n_h = bb_r.shape[-1]
    per_step = min(per_step, n_b)
    lhs = pl.BlockSpec((per_step, rows, n_p), lambda i: (i, 0, 0))
    rhs = pl.BlockSpec((per_step, n_p, n_h), lambda i: (i, 0, 0))
    return pl.pallas_call(
        _taps_kernel,
        grid=(n_b // per_step,),
        in_specs=[lhs, lhs, rhs, rhs],
        out_specs=pl.BlockSpec((per_step, rows, n_h), lambda i: (i, 0, 0)),
        out_shape=jax.ShapeDtypeStruct((n_b, rows, n_h), F32),
        compiler_params=_cparams(("arbitrary",)),
        name="s5_taps",
    )(cp_r, cp_i, bb_r, bb_i)


def _s5_weights(lam_re, lam_im, log_dt, b_re, b_im, c_re, c_im, d_skip, glu_w, glu_b, n_steps):
    L = S5_CHUNK
    n_g, n_p = lam_re.shape[1], lam_re.shape[2]
    n_h = b_re.shape[-1]
    lr = jnp.minimum(lam_re.astype(F32), DT_CLAMP)
    li = lam_im.astype(F32)
    dt = jnp.exp(log_dt.astype(F32))[..., None]
    mag = jnp.exp(lr * dt)
    ar = mag * jnp.cos(li * dt)
    ai = mag * jnp.sin(li * dt)
    den = lr * lr + li * li
    nr = ar - 1.0
    cr = (nr * lr + ai * li) / den
    ci = (ai * lr - nr * li) / den
    bbr = cr[..., None] * b_re - ci[..., None] * b_im
    bbi = cr[..., None] * b_im + ci[..., None] * b_re
    pr, pi = [jnp.ones_like(ar)], [jnp.zeros_like(ai)]
    for _ in range(L):
        pr_n = pr[-1] * ar - pi[-1] * ai
        pi_n = pr[-1] * ai + pi[-1] * ar
        pr.append(pr_n)
        pi.append(pi_n)
    pw_r = jnp.stack(pr)
    pw_i = jnp.stack(pi)
    cre = c_re.astype(F32)[None]
    cim = c_im.astype(F32)[None]
    cp_r = cre * pw_r[:, :, :, None, :] - cim * pw_i[:, :, :, None, :]
    cp_i = cre * pw_i[:, :, :, None, :] + cim * pw_r[:, :, :, None, :]
    to_rows = lambda c: c[:L].transpose(1, 2, 0, 3, 4).reshape(2 * n_g, L * n_h, n_p)
    taps = _s5_taps(to_rows(cp_r), to_rows(cp_i), bbr.reshape(2 * n_g, n_p, n_h), bbi.reshape(2 * n_g, n_p, n_h))
    kern = taps.reshape(2, n_g, L, n_h, n_h).transpose(2, 0, 1, 3, 4)
    kf, kb = kern[:, 0], kern[:, 1]
    centre = kf[0] + kb[0] + d_skip.astype(F32)[:, :, None] * jnp.eye(n_h, dtype=F32)
    ktab = jnp.concatenate([kb[1:][::-1], centre[None], kf[1:]], axis=0)
    idx = np.arange(L)[:, None] - np.arange(L)[None, :] + L - 1
    m_full = ktab[idx]
    mt = m_full.transpose(2, 1, 4, 0, 3).reshape(n_g, L * n_h, L * n_h)

    def win_part(d, order):
        wr = pw_r[order, d][:, :, :, None] * bbr[d][None] - pw_i[order, d][:, :, :, None] * bbi[d][None]
        wi = pw_r[order, d][:, :, :, None] * bbi[d][None] + pw_i[order, d][:, :, :, None] * bbr[d][None]
        tr = lambda w: w.transpose(1, 0, 3, 2).reshape(n_g, L * n_h, n_p)
        return tr(wr), tr(wi)

    wf_r, wf_i = win_part(0, np.arange(L - 1, -1, -1))
    wb_r, wb_i = win_part(1, np.arange(L))
    win = jnp.concatenate([wf_r, wf_i, wb_r, wb_i], axis=-1)

    def wout_part(d, order):
        er = cp_r[order, d]
        ei = cp_i[order, d]
        tr = lambda w: w.transpose(1, 3, 0, 2).reshape(n_g, n_p, L * n_h)
        return tr(er), tr(-ei)

    of_r, of_i = wout_part(0, np.arange(1, L + 1))
    ob_r, ob_i = wout_part(1, np.arange(L, 0, -1))
    wout = jnp.concatenate([of_r, of_i, ob_r, ob_i], axis=1)

    glu = jnp.einsum('ts,ghk->gthsk', jnp.eye(L, dtype=F32), glu_w.astype(F32)).reshape(n_g, L * n_h, L * n_h)
    glub = jnp.tile(glu_b.astype(F32), (1, L)).reshape(n_g, 1, L * n_h)

    qr, qi = pw_r[L], pw_i[L]
    rows = []
    for _ in range(n_steps):
        rows.append((jnp.concatenate([qr, qr], -1), jnp.concatenate([-qi, qi], -1)))
        qr, qi = qr * qr - qi * qi, 2.0 * qr * qi
    coef = jnp.stack([jnp.stack([a1, a2], axis=0) for a1, a2 in rows], axis=0)
    coef = coef.transpose(3, 2, 0, 1, 4).reshape(n_g, 2 * n_steps * 2, 2 * n_p)
    return mt.astype(BF16), win.astype(BF16), wout.astype(BF16), glu.astype(BF16), glub, coef


def _s5_kernel(u_ref, perm_ref, mt_ref, win_ref, wout_ref, glu_ref, glub_ref, coef_ref, o_ref, *, n_steps):
    L = S5_CHUNK
    n_c = u_ref.shape[1] // L
    gpb, lh = mt_ref.shape[0], mt_ref.shape[1]
    ww = jnp.concatenate([u_ref.at[0][pl.ds(t, n_c, stride=L), :].astype(BF16) for t in range(L)], axis=1)
    xp = jnp.dot(ww, perm_ref[...], preferred_element_type=F32).astype(BF16)
    row = lax.broadcasted_iota(jnp.int32, (n_c, 2 * S5_STATE), 0)

    def scan(x, g, d):
        for k in range(n_steps):
            sh = 1 << k
            if d == 0:
                shifted = jnp.where(row >= sh, pltpu.roll(x, sh, 0), 0.0)
            else:
                shifted = jnp.where(row < n_c - sh, pltpu.roll(x, n_c - sh, 0), 0.0)
            base = (d * n_steps + k) * 2
            a1 = coef_ref[g, base:base + 1, :]
            a2 = coef_ref[g, base + 1:base + 2, :]
            x = x + a1 * shifted + a2 * pltpu.roll(shifted, S5_STATE, 1)
        if d == 0:
            return jnp.where(row >= 1, pltpu.roll(x, 1, 0), 0.0)
        return jnp.where(row < n_c - 1, pltpu.roll(x, n_c - 1, 0), 0.0)

    outs = []
    for g in range(gpb):
        u = xp[:, g * lh:(g + 1) * lh]
        st = jnp.dot(u, win_ref[g], preferred_element_type=F32)
        two_p = st.shape[1] // 2
        hcat = jnp.concatenate([scan(st[:, :two_p], g, 0), scan(st[:, two_p:], g, 1)], axis=1).astype(BF16)
        y = (jnp.dot(u, mt_ref[g], preferred_element_type=F32)
             + jnp.dot(hcat, wout_ref[g], preferred_element_type=F32))
        y = _gelu(y)
        gate = jax.nn.sigmoid(jnp.dot(y.astype(BF16), glu_ref[g], preferred_element_type=F32) + glub_ref[g])
        outs.append((y * gate).astype(BF16))
    back = lax.dot_general(jnp.concatenate(outs, axis=1), perm_ref[...], (((1,), (1,)), ((), ())),
                           preferred_element_type=F32)
    for t in range(L):
        o_ref.at[0][pl.ds(t, n_c, stride=L), :] = back[:, t * LANES:(t + 1) * LANES]


def _s5(ua, weights, n_steps):
    mt, win, wout, glu, glub, coef = weights
    bsz, s_len, width = ua.shape
    n_g = mt.shape[0]
    n_h = width // n_g
    L = S5_CHUNK
    gpb = LANES // n_h
    n_lb = width // LANES
    cols = L * LANES
    ci = np.arange(cols)
    t_i, g_i, h_i = ci // LANES, (ci % LANES) // n_h, ci % n_h
    perm = (jnp.asarray(g_i * (L * n_h) + t_i * n_h + h_i)[:, None] == jnp.arange(cols)[None, :]).astype(BF16)
    blk = lambda a: pl.BlockSpec((gpb,) + a.shape[1:], lambda b, lb: (lb, 0, 0))
    return pl.pallas_call(
        functools.partial(_s5_kernel, n_steps=n_steps),
        grid=(bsz, n_lb),
        in_specs=[pl.BlockSpec((1, s_len, LANES), lambda b, lb: (b, 0, lb)),
                  pl.BlockSpec((cols, cols), lambda b, lb: (0, 0)),
                  blk(mt), blk(win), blk(wout), blk(glu), blk(glub), blk(coef)],
        out_specs=pl.BlockSpec((1, s_len, LANES), lambda b, lb: (b, 0, lb)),
        out_shape=jax.ShapeDtypeStruct((bsz, s_len, width), F32),
        compiler_params=_cparams(("arbitrary", "arbitrary")),
        name="s5",
    )(ua, perm, mt, win, wout, glu, glub, coef)


def _attn_kernel(q_ref, k_ref, v_ref, o_ref, *, tk):
    tq = q_ref.shape[1]
    qs = jnp.concatenate([q_ref[0, :, h * HEAD_DIM:(h + 1) * HEAD_DIM] for h in range(GQA_GROUP)], axis=0)
    n_rows = qs.shape[0]
    n_kv = k_ref.shape[1] // tk

    def body(j, carry):
        m, l, acc = carry
        start = pl.multiple_of(j * tk, tk)
        kc = k_ref[0, pl.ds(start, tk), :]
        vc = v_ref[0, pl.ds(start, tk), :]
        s = lax.dot_general(qs, kc, (((1,), (1,)), ((), ())), preferred_element_type=F32)
        m_new = jnp.maximum(m, jnp.max(s, axis=-1, keepdims=True))
        p = jnp.exp(s - m_new)
        alpha = jnp.exp(m - m_new)
        l = alpha * l + jnp.sum(p, axis=-1, keepdims=True)
        acc = alpha * acc + jnp.dot(p.astype(BF16), vc, preferred_element_type=F32)
        return m_new, l, acc

    m0 = jnp.full((n_rows, 1), -jnp.inf, F32)
    l0 = jnp.zeros((n_rows, 1), F32)
    a0 = jnp.zeros((n_rows, HEAD_DIM), F32)
    _, l, acc = lax.fori_loop(0, n_kv, body, (m0, l0, a0), unroll=True)
    out = acc / l
    for h in range(GQA_GROUP):
        o_ref[0, :, h * HEAD_DIM:(h + 1) * HEAD_DIM] = out[h * tq:(h + 1) * tq].astype(BF16)


def _attention(q, k, v, tq=128, tk=2048):
    bsz, s_len, n_q = q.shape
    gw = GQA_GROUP * HEAD_DIM
    tk = min(tk, s_len)
    return pl.pallas_call(
        functools.partial(_attn_kernel, tk=tk),
        grid=(bsz, N_KV_HEADS, s_len // tq),
        in_specs=[pl.BlockSpec((1, tq, gw), lambda b, h, i: (b, i, h)),
                  pl.BlockSpec((1, s_len, HEAD_DIM), lambda b, h, i: (b, 0, h)),
                  pl.BlockSpec((1, s_len, HEAD_DIM), lambda b, h, i: (b, 0, h))],
        out_specs=pl.BlockSpec((1, tq, gw), lambda b, h, i: (b, i, h)),
        out_shape=jax.ShapeDtypeStruct((bsz, s_len, n_q), BF16),
        compiler_params=_cparams(("arbitrary", "arbitrary", "arbitrary")),
        name="attention",
    )(q, k, v)


def _mid0_kernel(a_ref, b_ref, wa_ref, wb_ref, x_ref, g1_ref, g_ref, sc_ref, sh_ref, x1_ref, ht_ref):
    m = (jnp.dot(a_ref[0].astype(BF16), wa_ref[...], preferred_element_type=F32)
         + jnp.dot(b_ref[0], wb_ref[...], preferred_element_type=F32))
    x1 = x_ref[0] + g1_ref[0] * m
    x1_ref[0] = x1
    h2 = _ada_norm(x1, g_ref[...], sc_ref[0], sh_ref[0])
    ht_ref[...] = h2.T.astype(BF16)


def _mid0(a_out, b_out, w_a, w_b, x, g1, g, sc, sh, tm=256):
    bsz, s_len, d = x.shape
    n_t = s_len // tm
    wa_n, wb_n = w_a.shape[0], w_b.shape[0]
    vec = lambda: pl.BlockSpec((1, 1, d), lambda b, i: (b, 0, 0))
    return pl.pallas_call(
        _mid0_kernel,
        grid=(bsz, n_t),
        in_specs=[pl.BlockSpec((1, tm, wa_n), lambda b, i: (b, i, 0)),
                  pl.BlockSpec((1, tm, wb_n), lambda b, i: (b, i, 0)),
                  pl.BlockSpec((wa_n, d), lambda b, i: (0, 0)),
                  pl.BlockSpec((wb_n, d), lambda b, i: (0, 0)),
                  pl.BlockSpec((1, tm, d), lambda b, i: (b, i, 0)),
                  vec(),
                  pl.BlockSpec((1, d), lambda b, i: (0, 0)),
                  vec(), vec()],
        out_specs=[pl.BlockSpec((1, tm, d), lambda b, i: (b, i, 0)),
                   pl.BlockSpec((d, tm), lambda b, i: (0, b * n_t + i))],
        out_shape=[jax.ShapeDtypeStruct((bsz, s_len, d), F32),
                   jax.ShapeDtypeStruct((d, bsz * s_len), BF16)],
        compiler_params=_cparams(("arbitrary", "arbitrary")),
        name="mid0",
    )(a_out, b_out, w_a, w_b, x, g1, g, sc, sh)


def _odd_kernel(xp_ref, pt_ref, g2p_ref, g_ref, sc_ref, sh_ref, wu_ref, wv_ref, lng_ref, ws_ref, bs_ref, wo_ref,
                g1_ref, g2n_ref, sc2_ref, sh2_ref, x1_ref, ht_ref, h_scr, acc_scr):
    gi = pl.program_id(2)
    n_groups = pl.num_programs(2)

    @pl.when(gi == 0)
    def _():
        x = xp_ref[0] + g2p_ref[0] * pt_ref[...].T
        x1_ref[0] = x
        h_scr[...] = _ada_norm(x, g_ref[...], sc_ref[0], sh_ref[0]).astype(BF16)
        acc_scr[...] = jnp.zeros_like(acc_scr)

    h = h_scr[...]
    zu = _gelu(jnp.dot(h, wu_ref[...], preferred_element_type=F32))
    zv = _gelu(jnp.dot(h, wv_ref[...], preferred_element_type=F32))
    mu = jnp.mean(zv, axis=-1, keepdims=True)
    cen = zv - mu
    var = jnp.mean(cen * cen, axis=-1, keepdims=True)
    vn = (cen * lax.rsqrt(var + EPS) * lng_ref[0]).astype(BF16)
    ws = ws_ref[0]
    bias = bs_ref[0]
    tm = zu.shape[0]
    gated = []
    for c in range(tm // SGU_CHUNK):
        sl = slice(c * SGU_CHUNK, (c + 1) * SGU_CHUNK)
        s = jnp.dot(ws, vn[sl], preferred_element_type=F32) + bias
        gated.append((zu[sl] * s).astype(BF16))
    gated = jnp.concatenate(gated, axis=0)
    acc_scr[...] += jnp.dot(gated, wo_ref[...], preferred_element_type=F32)

    @pl.when(gi == n_groups - 1)
    def _():
        x1 = x1_ref[0] + g1_ref[0] * acc_scr[...]
        x1_ref[0] = x1
        h2 = _ada_norm(x1, g2n_ref[...], sc2_ref[0], sh2_ref[0])
        ht_ref[...] = h2.T.astype(BF16)


def _odd(xp, peer_t, g2p, g, sc, sh, w_in, ln_g, w_s, b_s, w_out, g1, g2n, sc2, sh2, tm=512):
    bsz, s_len, d = xp.shape
    n_t = s_len // tm
    width = w_out.shape[0]
    gw = width // SGU_GROUPS
    vec = lambda: pl.BlockSpec((1, 1, d), lambda b, i, gi: (b, 0, 0))
    row = lambda: pl.BlockSpec((1, d), lambda b, i, gi: (0, 0))
    bias = jnp.broadcast_to(b_s.astype(F32)[:, :, None], (SGU_GROUPS, SGU_CHUNK, gw))
    return pl.pallas_call(
        _odd_kernel,
        grid=(bsz, n_t, SGU_GROUPS),
        in_specs=[pl.BlockSpec((1, tm, d), lambda b, i, gi: (b, i, 0)),
                  pl.BlockSpec((d, tm), lambda b, i, gi: (0, b * n_t + i)),
                  vec(),
                  row(), vec(), vec(),
                  pl.BlockSpec((d, gw), lambda b, i, gi: (0, gi)),
                  pl.BlockSpec((d, gw), lambda b, i, gi: (0, SGU_GROUPS + gi)),
                  pl.BlockSpec((1, 1, gw), lambda b, i, gi: (gi, 0, 0)),
                  pl.BlockSpec((1, SGU_CHUNK, SGU_CHUNK), lambda b, i, gi: (gi, 0, 0)),
                  pl.BlockSpec((1, SGU_CHUNK, gw), lambda b, i, gi: (gi, 0, 0)),
                  pl.BlockSpec((gw, d), lambda b, i, gi: (gi, 0)),
                  vec(), row(), vec(), vec()],
        out_specs=[pl.BlockSpec((1, tm, d), lambda b, i, gi: (b, i, 0)),
                   pl.BlockSpec((d, tm), lambda b, i, gi: (0, b * n_t + i))],
        out_shape=[jax.ShapeDtypeStruct((bsz, s_len, d), F32),
                   jax.ShapeDtypeStruct((d, bsz * s_len), BF16)],
        scratch_shapes=[pltpu.VMEM((tm, d), BF16), pltpu.VMEM((tm, d), F32)],
        compiler_params=_cparams(("arbitrary", "arbitrary", "arbitrary")),
        name="odd_mixer",
    )(xp, peer_t, g2p, g, sc, sh, w_in, w_in, ln_g.reshape(SGU_GROUPS, 1, gw), w_s, bias, w_out, g1, g2n, sc2, sh2)


def _compose_kernel(k_ref, w_ref, o_ref):
    o_ref[0, 0] = lax.dot_general(k_ref[0], w_ref[...], (((1,), (1,)), ((), ())),
                                  preferred_element_type=F32, precision=HP)


def _compose_score_weights(w_q, k1, k2):
    d = w_q.shape[0]
    dk = k1.shape[1]
    keys = jnp.stack([k1, k2]).astype(F32)
    out = pl.pallas_call(
        _compose_kernel,
        grid=(2, PEER_HEADS),
        in_specs=[pl.BlockSpec((1, PEER_NKEYS, dk), lambda a, h: (a, 0, 0)),
                  pl.BlockSpec((d, dk), lambda a, h: (0, h * 2 + a))],
        out_specs=pl.BlockSpec((1, 1, PEER_NKEYS, d), lambda a, h: (a, h, 0, 0)),
        out_shape=jax.ShapeDtypeStruct((2, PEER_HEADS, PEER_NKEYS, d), F32),
        compiler_params=_cparams(("arbitrary", "arbitrary")),
        name="compose_scores",
    )(keys, w_q.astype(F32))
    return out.transpose(0, 2, 1, 3).reshape(2 * PEER_NKEYS * PEER_HEADS, d).astype(BF16)


def _batcher_pairs(n):
    pairs = []
    p = 1
    while p < n:
        k = p
        while k >= 1:
            for j in range(k % p, n - k, 2 * k):
                for i in range(min(k, n - j - k)):
                    if (i + j) // (2 * p) == (i + j + k) // (2 * p):
                        pairs.append((i + j, i + j + k))
            k //= 2
        p *= 2
    return pairs


_SORT16 = _batcher_pairs(PEER_TOPK)


def _vmax(a, b):
    if a is None:
        return b
    if b is None:
        return a
    return jnp.maximum(a, b)


def _vmin(a, b):
    if a is None or b is None:
        return None
    return jnp.minimum(a, b)


def _sort16_desc(v):
    v = list(v)
    for i, j in _SORT16:
        v[i], v[j] = _vmax(v[i], v[j]), _vmin(v[i], v[j])
    return v


def _merge_top16(a, b):
    n = PEER_TOPK
    c = [_vmax(a[i], b[n - 1 - i]) for i in range(n)]
    d = n // 2
    while d >= 1:
        for i in range(n):
            if (i & d) == 0:
                c[i], c[i + d] = _vmax(c[i], c[i + d]), _vmin(c[i], c[i + d])
        d //= 2
    return c


def _top16_sorted(vals):
    groups = [_sort16_desc(vals[i:i + PEER_TOPK]) for i in range(0, len(vals), PEER_TOPK)]
    while len(groups) > 1:
        groups = [_merge_top16(groups[i], groups[i + 1]) for i in range(0, len(groups), 2)]
    return groups[0]


def _route_kernel(ht_ref, w_ref, cnt_ref, p1_ref, r2_ref, p2_ref, r2_scr, p2_scr, *, tw):
    for sp in range(ht_ref.shape[1] // tw):
        _route_block(ht_ref, w_ref, cnt_ref, p1_ref, r2_ref, p2_ref, r2_scr, p2_scr, sp * tw, tw)


def _route_block(ht_ref, w_ref, cnt_ref, p1_ref, r2_ref, p2_ref, r2_scr, p2_scr, lane0, tw):
    nh, nk = PEER_HEADS, PEER_NKEYS
    tok = slice(lane0, lane0 + tw)
    lb0 = lane0 // LANES
    st = jnp.dot(w_ref[...], ht_ref[:, tok], preferred_element_type=F32)
    s1 = [st[i * nh:(i + 1) * nh] for i in range(nk)]
    s2 = [st[(nk + i) * nh:(nk + i + 1) * nh] for i in range(nk)]
    a = _top16_sorted(s1)
    b = _top16_sorted(s2)
    cands = [[a[k] + b[l] for l in range(PEER_TOPK // (k + 1))] for k in range(PEER_TOPK)]
    flat = [c for rowc in cands[1:] for c in rowc]
    flat = flat + [None] * (-len(flat) % PEER_TOPK)
    groups = [cands[0]] + [_sort16_desc(flat[i:i + PEER_TOPK]) for i in range(0, len(flat), PEER_TOPK)]
    top = groups[0]
    for gr in groups[1:]:
        top = _merge_top16(top, gr)
    theta = top[PEER_TOPK - 1]
    z = jnp.ones_like(theta)
    for t in top[1:]:
        z = z + jnp.exp(t - top[0])
    inv_z = 1.0 / z
    n_lb = tw // LANES
    cnt_k = []
    for k in range(PEER_TOPK):
        c = jnp.zeros_like(theta)
        for cand in cands[k]:
            c = c + jnp.where(cand >= theta, 1.0, 0.0)
        cnt_k.append(c)
    for i in range(nk):
        rows = slice(i * nh, (i + 1) * nh)
        cnt = jnp.zeros_like(theta)
        rank = jnp.full_like(theta, float(PEER_TOPK))
        for l in range(PEER_TOPK - 1, -1, -1):
            cnt = jnp.where(s1[i] >= a[l], cnt_k[l], cnt)
            rank = jnp.where(s2[i] >= b[l], float(l), rank)
        cnt_ref[rows, tok] = cnt
        p1_ref[rows, tok] = jnp.exp(s1[i] - a[0]) * inv_z
        p2 = jnp.exp(s2[i] - b[0])
        for lb in range(n_lb):
            lanes = slice(lb * LANES, (lb + 1) * LANES)
            r2_scr[lb0 + lb, rows, :] = rank[:, lanes]
            p2_scr[lb0 + lb, rows, :] = p2[:, lanes]
    for h in range(nh):
        for lb in range(lb0, lb0 + n_lb):
            lanes = slice(lb * LANES, (lb + 1) * LANES)
            r2_ref[h, :, lanes] = r2_scr.at[lb][pl.ds(h, nk, stride=nh), :].astype(BF16)
            p2_ref[h, :, lanes] = p2_scr.at[lb][pl.ds(h, nk, stride=nh), :].astype(BF16)


def _route(ht, w_s, tm=512, tw=256):
    d, t = ht.shape
    nh, nk = PEER_HEADS, PEER_NKEYS
    rows = nh * nk
    tw = min(tw, tm)
    return pl.pallas_call(
        functools.partial(_route_kernel, tw=tw),
        grid=(t // tm,),
        in_specs=[pl.BlockSpec((d, tm), lambda i: (0, i)),
                  pl.BlockSpec((2 * rows, d), lambda i: (0, 0))],
        out_specs=[pl.BlockSpec((rows, tm), lambda i: (0, i)),
                   pl.BlockSpec((rows, tm), lambda i: (0, i)),
                   pl.BlockSpec((nh, nk, tm), lambda i: (0, 0, i)),
                   pl.BlockSpec((nh, nk, tm), lambda i: (0, 0, i))],
        out_shape=[jax.ShapeDtypeStruct((rows, t), F32),
                   jax.ShapeDtypeStruct((rows, t), F32),
                   jax.ShapeDtypeStruct((nh, nk, t), BF16),
                   jax.ShapeDtypeStruct((nh, nk, t), BF16)],
        scratch_shapes=[pltpu.VMEM((tm // LANES, rows, LANES), F32), pltpu.VMEM((tm // LANES, rows, LANES), F32)],
        compiler_params=_cparams(("arbitrary",)),
        name="peer_route",
    )(ht, w_s)


def _peer_kernel(ht_ref, u_ref, vt_ref, cnt_ref, p1_ref, r2_ref, p2_ref, o_ref, *, n_split):
    ei = pl.program_id(1)
    nh, nk = PEER_HEADS, PEER_NKEYS
    te = u_ref.shape[0]
    ts = te // n_split

    @pl.when(ei == 0)
    def _():
        o_ref[...] = jnp.zeros_like(o_ref)

    gates = []
    for ii in range(te // nk):
        key_i = ei * (te // nk) + ii
        gate = None
        for h in range(nh):
            r = key_i * nh + h
            cnt = cnt_ref[pl.ds(r, 1), :].astype(BF16)
            p1 = p1_ref[pl.ds(r, 1), :].astype(BF16)
            p2 = p2_ref[h]
            term = jnp.where(r2_ref[h] < cnt, p2, jnp.zeros_like(p2)) * p1
            gate = term if gate is None else gate + term
        gates.append(gate)

    wts = []
    for sp in range(n_split):
        act = jnp.dot(u_ref[sp * ts:(sp + 1) * ts, :], ht_ref[...], preferred_element_type=F32)
        gate = jnp.concatenate(gates[sp * (ts // nk):(sp + 1) * (ts // nk)], axis=0)
        wts.append(_gelu(act.astype(BF16)) * gate)
    wt = jnp.concatenate(wts, axis=0)
    o_ref[...] += jnp.dot(vt_ref[...], wt, preferred_element_type=F32)


def _peer(ht, u_tabs, vt_tabs, layer, cnt, p1, r2, p2, tm=512, te=1024, n_split=8):
    d, t = ht.shape
    n_e = u_tabs.shape[1]
    nh, nk = PEER_HEADS, PEER_NKEYS
    rows = nh * nk
    return pl.pallas_call(
        functools.partial(_peer_kernel, n_split=n_split),
        grid=(t // tm, n_e // te),
        in_specs=[pl.BlockSpec((d, tm), lambda i, e: (0, i)),
                  pl.BlockSpec((None, te, d), lambda i, e: (layer, e, 0)),
                  pl.BlockSpec((None, d, te), lambda i, e: (layer, 0, e)),
                  pl.BlockSpec((rows, tm), lambda i, e: (0, i)),
                  pl.BlockSpec((rows, tm), lambda i, e: (0, i)),
                  pl.BlockSpec((nh, nk, tm), lambda i, e: (0, 0, i)),
                  pl.BlockSpec((nh, nk, tm), lambda i, e: (0, 0, i))],
        out_specs=pl.BlockSpec((d, tm), lambda i, e: (0, i)),
        out_shape=jax.ShapeDtypeStruct((d, t), F32),
        compiler_params=_cparams(("arbitrary", "arbitrary")),
        name="peer_dense",
    )(ht, u_tabs, vt_tabs, cnt, p1, r2, p2)


def _resid_kernel(x_ref, pt_ref, g2_ref, fg_ref, o_ref, *, final_norm):
    x2 = x_ref[0] + g2_ref[0] * pt_ref[...].T
    if final_norm:
        ms = jnp.mean(x2 * x2, axis=-1, keepdims=True)
        x2 = x2 * lax.rsqrt(ms + EPS) * fg_ref[...]
    o_ref[0] = x2


def _resid(x1, peer_t, g2, final_g, final_norm, tm=256):
    bsz, s_len, d = x1.shape
    n_t = s_len // tm
    return pl.pallas_call(
        functools.partial(_resid_kernel, final_norm=final_norm),
        grid=(bsz, n_t),
        in_specs=[pl.BlockSpec((1, tm, d), lambda b, i: (b, i, 0)),
                  pl.BlockSpec((d, tm), lambda b, i: (0, b * n_t + i)),
                  pl.BlockSpec((1, 1, d), lambda b, i: (b, 0, 0)),
                  pl.BlockSpec((1, d), lambda b, i: (0, 0))],
        out_specs=pl.BlockSpec((1, tm, d), lambda b, i: (b, i, 0)),
        out_shape=jax.ShapeDtypeStruct((bsz, s_len, d), F32),
        compiler_params=_cparams(("arbitrary", "arbitrary")),
        name="peer_residual",
    )(x1, peer_t, g2, final_g)


def _peer_layer(ht, w_q, k1, k2, u_tabs, vt_tabs, layer):
    w_s = _compose_score_weights(w_q, k1, k2)
    cnt, p1, r2, p2 = _route(ht, w_s)
    return _peer(ht, u_tabs, vt_tabs, layer, cnt, p1, r2, p2)


def _rope_tables(s_len):
    rows = s_len // GRID_W
    row = jnp.repeat(jnp.arange(rows, dtype=F32), GRID_W)
    col = jnp.tile(jnp.arange(GRID_W, dtype=F32), rows)
    n_pair_axis = HEAD_DIM // 4
    inv_freq = ROPE_THETA ** (-jnp.arange(n_pair_axis, dtype=F32) / n_pair_axis)
    ang = jnp.concatenate([row[:, None] * inv_freq, col[:, None] * inv_freq], axis=-1)
    cos, sin = jnp.cos(ang), jnp.sin(ang)
    return jnp.concatenate([cos, cos], axis=-1), jnp.concatenate([-sin, sin], axis=-1)


def _deinterleave_perm():
    half = HEAD_DIM // 2
    return np.concatenate([2 * np.arange(half), 2 * np.arange(half) + 1])


def kernel(x, c, ada_w, ada_b, norm_g, even_w_in, even_w_out, s5_lam_re, s5_lam_im, s5_log_dt, s5_b_re, s5_b_im, s5_c_re, s5_c_im, s5_d, s5_glu_w, s5_glu_b, q_norm_g, k_norm_g, odd_w_in, odd_w_out, sgu_ln_g, sgu_w, sgu_b, peer_w_q, peer_k1, peer_k2, peer_u, peer_v, final_g):
    bsz, s_len, d = x.shape
    depth = ada_w.shape[0]
    mod = _modulation(c, ada_w, ada_b)
    fg = final_g.reshape(1, d)
    n_a = s5_lam_re.shape[2] * S5_GROUP
    n_q = N_Q_HEADS * HEAD_DIM
    n_kv = N_KV_HEADS * HEAD_DIM
    n_steps = max(1, int(math.log2(s_len // S5_CHUNK)))
    cos_t, sin_t = _rope_tables(s_len)
    perm = _deinterleave_perm()
    u_tabs = peer_u.astype(BF16)
    vt_tabs = jnp.swapaxes(peer_v, 1, 2).astype(BF16)

    pending = None
    for layer in range(depth):
        sh1, sc1, g1, sh2, sc2, g2 = [m.reshape(bsz, 1, d) for m in jnp.split(mod[layer], 6, axis=-1)]
        ng1 = norm_g[layer, 0].reshape(1, d)
        ng2 = norm_g[layer, 1].reshape(1, d)
        i = layer // 2
        if layer % 2 == 0:
            if pending is not None:
                x = _resid(*pending, fg, False)
            w_in = even_w_in[i]
            col = np.arange(w_in.shape[1])
            for hd in range(N_Q_HEADS + N_KV_HEADS):
                lo = n_a + hd * HEAD_DIM
                col[lo:lo + HEAD_DIM] = lo + perm
            w_in_p = w_in[:, col].astype(BF16)
            gq = (q_norm_g[i][perm] * (HEAD_DIM ** -0.5)).reshape(1, HEAD_DIM)
            gk = k_norm_g[i][perm].reshape(1, HEAD_DIM)
            ua, q, k, v = _front0(x, ng1, sc1, sh1, w_in_p, gq, gk, cos_t, sin_t, n_a, n_q, n_kv)
            weights = _s5_weights(s5_lam_re[i], s5_lam_im[i], s5_log_dt[i], s5_b_re[i], s5_b_im[i],
                                  s5_c_re[i], s5_c_im[i], s5_d[i], s5_glu_w[i], s5_glu_b[i], n_steps)
            a_out = _s5(ua, weights, n_steps)
            b_out = _attention(q, k, v)
            w_out = even_w_out[i].astype(BF16)
            x1, ht = _mid0(a_out, b_out, w_out[:n_a], w_out[n_a:], x, g1, ng2, sc2, sh2)
        else:
            x1, ht = _odd(*pending, ng1, sc1, sh1, odd_w_in[i].astype(BF16), sgu_ln_g[i], sgu_w[i].astype(BF16),
                          sgu_b[i], odd_w_out[i].astype(BF16), g1, ng2, sc2, sh2)
        peer_t = _peer_layer(ht, peer_w_q[layer], peer_k1[layer], peer_k2[layer], u_tabs, vt_tabs, layer)
        pending = (x1, peer_t, g2)
    return _resid(*pending, fg, True)
```

```python
import functools
import math

import numpy as np
import jax
import jax.numpy as jnp
from jax import lax
from jax.experimental import pallas as pl
from jax.experimental.pallas import tpu as pltpu

F32 = jnp.float32
BF16 = jnp.bfloat16
FP8 = jnp.float8_e4m3fn
FP8_TARGET = 224.0
FP8_TINY = 1e-30
EPS = 1e-6
HP = lax.Precision.HIGHEST

V7X_VMEM_LIMIT_BYTES = 56 * 1024 * 1024
LANES = 128
SUBLANES = 8

GRID_W = 64
S5_GROUP = 16
S5_STATE = 64
S5_CHUNK = 16
HEAD_DIM = 128
N_Q_HEADS = 8
N_KV_HEADS = 2
GQA_GROUP = N_Q_HEADS // N_KV_HEADS
ROPE_THETA = 10000.0
SGU_CHUNK = 128
SGU_GROUPS = 8
PEER_HEADS = 8
PEER_NKEYS = 128
PEER_TOPK = 16
DT_CLAMP = -1e-4


def _cparams(semantics):
    return pltpu.CompilerParams(dimension_semantics=semantics, vmem_limit_bytes=V7X_VMEM_LIMIT_BYTES)


def _gelu(x):
    return jax.nn.gelu(x)


def _mod_kernel(c_ref, w_ref, b_ref, o_ref):
    c = c_ref[...]
    ca = c * jax.nn.sigmoid(c)
    o_ref[0] = jnp.dot(ca, w_ref[0], preferred_element_type=F32, precision=HP) + b_ref[0]


def _modulation(c, ada_w, ada_b):
    depth, d, n = ada_w.shape
    bsz = c.shape[0]
    cpad = jnp.zeros((SUBLANES, d), F32).at[:bsz].set(c)
    tn = 1024
    out = pl.pallas_call(
        _mod_kernel,
        grid=(depth, n // tn),
        in_specs=[pl.BlockSpec((SUBLANES, d), lambda l, j: (0, 0)),
                  pl.BlockSpec((1, d, tn), lambda l, j: (l, 0, j)),
                  pl.BlockSpec((1, 1, tn), lambda l, j: (l, 0, j))],
        out_specs=pl.BlockSpec((1, SUBLANES, tn), lambda l, j: (l, 0, j)),
        out_shape=jax.ShapeDtypeStruct((depth, SUBLANES, n), F32),
        compiler_params=_cparams(("arbitrary", "arbitrary")),
        name="modulation",
    )(cpad, ada_w, ada_b.reshape(depth, 1, n))
    return out[:, :bsz]


def _ada_norm(x, g, sc, sh):
    ms = jnp.mean(x * x, axis=-1, keepdims=True)
    return x * lax.rsqrt(ms + EPS) * (g * (1.0 + sc)) + sh


def _front0_kernel(x_ref, g_ref, sc_ref, sh_ref, w_ref, gq_ref, gk_ref, cos_ref, sin_ref,
                   ua_ref, q_ref, k_ref, v_ref, *, n_a, n_q, n_kv):
    h = _ada_norm(x_ref[0], g_ref[...], sc_ref[0], sh_ref[0])
    z = jnp.dot(h.astype(BF16), w_ref[...], preferred_element_type=F32)
    ua_ref[0] = z[:, :n_a]
    cos = cos_ref[...]
    sin = sin_ref[...]

    def norm_rope(y, g):
        y = y * lax.rsqrt(jnp.mean(y * y, axis=-1, keepdims=True) + EPS) * g
        return y * cos + pltpu.roll(y, HEAD_DIM // 2, 1) * sin

    for hd in range(n_q // HEAD_DIM):
        lo = n_a + hd * HEAD_DIM
        q_ref[0, :, hd * HEAD_DIM:(hd + 1) * HEAD_DIM] = norm_rope(z[:, lo:lo + HEAD_DIM], gq_ref[...]).astype(BF16)
    for hd in range(n_kv // HEAD_DIM):
        lo = n_a + n_q + hd * HEAD_DIM
        k_ref[0, :, hd * HEAD_DIM:(hd + 1) * HEAD_DIM] = norm_rope(z[:, lo:lo + HEAD_DIM], gk_ref[...]).astype(BF16)
    v_ref[0] = z[:, n_a + n_q + n_kv:].astype(BF16)


def _front0(x, g, sc, sh, w_in, gq, gk, cos_t, sin_t, n_a, n_q, n_kv, tm=256):
    bsz, s_len, d = x.shape
    n_all = w_in.shape[1]
    vec = lambda: pl.BlockSpec((1, 1, d), lambda b, i: (b, 0, 0))
    return pl.pallas_call(
        functools.partial(_front0_kernel, n_a=n_a, n_q=n_q, n_kv=n_kv),
        grid=(bsz, s_len // tm),
        in_specs=[pl.BlockSpec((1, tm, d), lambda b, i: (b, i, 0)),
                  pl.BlockSpec((1, d), lambda b, i: (0, 0)),
                  vec(), vec(),
                  pl.BlockSpec((d, n_all), lambda b, i: (0, 0)),
                  pl.BlockSpec((1, HEAD_DIM), lambda b, i: (0, 0)),
                  pl.BlockSpec((1, HEAD_DIM), lambda b, i: (0, 0)),
                  pl.BlockSpec((tm, HEAD_DIM), lambda b, i: (i, 0)),
                  pl.BlockSpec((tm, HEAD_DIM), lambda b, i: (i, 0))],
        out_specs=[pl.BlockSpec((1, tm, n_a), lambda b, i: (b, i, 0)),
                   pl.BlockSpec((1, tm, n_q), lambda b, i: (b, i, 0)),
                   pl.BlockSpec((1, tm, n_kv), lambda b, i: (b, i, 0)),
                   pl.BlockSpec((1, tm, n_kv), lambda b, i: (b, i, 0))],
        out_shape=[jax.ShapeDtypeStruct((bsz, s_len, n_a), F32),
                   jax.ShapeDtypeStruct((bsz, s_len, n_q), BF16),
                   jax.ShapeDtypeStruct((bsz, s_len, n_kv), BF16),
                   jax.ShapeDtypeStruct((bsz, s_len, n_kv), BF16)],
        compiler_params=_cparams(("arbitrary", "arbitrary")),
        name="front0",
    )(x, g, sc, sh, w_in, gq, gk, cos_t, sin_t)


def _taps_kernel(cr_ref, ci_ref, br_ref, bi_ref, o_ref):
    for j in range(cr_ref.shape[0]):
        o_ref[j] = (jnp.dot(cr_ref[j], br_ref[j], preferred_element_type=F32, precision=HP)
                    - jnp.dot(ci_ref[j], bi_ref[j], preferred_element_type=F32, precision=HP))


def _s5_taps(cp_r, cp_i, bb_r, bb_i, per_step=8):
    n_b, rows, n_p = cp_r.shape
    n_h = bb_r.shape[-1]
    per_step = min(per_step, n_b)
    lhs = pl.BlockSpec((per_step, rows, n_p), lambda i: (i, 0, 0))
    rhs = pl.BlockSpec((per_step, n_p, n_h), lambda i: (i, 0, 0))
    return pl.pallas_call(
        _taps_kernel,
        grid=(n_b // per_step,),
        in_specs=[lhs, lhs, rhs, rhs],
        out_specs=pl.BlockSpec((per_step, rows, n_h), lambda i: (i, 0, 0)),
        out_shape=jax.ShapeDtypeStruct((n_b, rows, n_h), F32),
        compiler_params=_cparams(("arbitrary",)),
        name="s5_taps",
    )(cp_r, cp_i, bb_r, bb_i)


def _s5_weights(lam_re, lam_im, log_dt, b_re, b_im, c_re, c_im, d_skip, glu_w, glu_b, n_steps):
    L = S5_CHUNK
    n_g, n_p = lam_re.shape[1], lam_re.shape[2]
    n_h = b_re.shape[-1]
    lr = jnp.minimum(lam_re.astype(F32), DT_CLAMP)
    li = lam_im.astype(F32)
    dt = jnp.exp(log_dt.astype(F32))[..., None]
    mag = jnp.exp(lr * dt)
    ar = mag * jnp.cos(li * dt)
    ai = mag * jnp.sin(li * dt)
    den = lr * lr + li * li
    nr = ar - 1.0
    cr = (nr * lr + ai * li) / den
    ci = (ai * lr - nr * li) / den
    bbr = cr[..., None] * b_re - ci[..., None] * b_im
    bbi = cr[..., None] * b_im + ci[..., None] * b_re
    pr, pi = [jnp.ones_like(ar)], [jnp.zeros_like(ai)]
    for _ in range(L):
        pr_n = pr[-1] * ar - pi[-1] * ai
        pi_n = pr[-1] * ai + pi[-1] * ar
        pr.append(pr_n)
        pi.append(pi_n)
    pw_r = jnp.stack(pr)
    pw_i = jnp.stack(pi)
    cre = c_re.astype(F32)[None]
    cim = c_im.astype(F32)[None]
    cp_r = cre * pw_r[:, :, :, None, :] - cim * pw_i[:, :, :, None, :]
    cp_i = cre * pw_i[:, :, :, None, :] + cim * pw_r[:, :, :, None, :]
    to_rows = lambda c: c[:L].transpose(1, 2, 0, 3, 4).reshape(2 * n_g, L * n_h, n_p)
    taps = _s5_taps(to_rows(cp_r), to_rows(cp_i), bbr.reshape(2 * n_g, n_p, n_h), bbi.reshape(2 * n_g, n_p, n_h))
    kern = taps.reshape(2, n_g, L, n_h, n_h).transpose(2, 0, 1, 3, 4)
    kf, kb = kern[:, 0], kern[:, 1]
    centre = kf[0] + kb[0] + d_skip.astype(F32)[:, :, None] * jnp.eye(n_h, dtype=F32)
    ktab = jnp.concatenate([kb[1:][::-1], centre[None], kf[1:]], axis=0)
    idx = np.arange(L)[:, None] - np.arange(L)[None, :] + L - 1
    m_full = ktab[idx]
    mt = m_full.transpose(2, 1, 4, 0, 3).reshape(n_g, L * n_h, L * n_h)

    def win_part(d, order):
        wr = pw_r[order, d][:, :, :, None] * bbr[d][None] - pw_i[order, d][:, :, :, None] * bbi[d][None]
        wi = pw_r[order, d][:, :, :, None] * bbi[d][None] + pw_i[order, d][:, :, :, None] * bbr[d][None]
        tr = lambda w: w.transpose(1, 0, 3, 2).reshape(n_g, L * n_h, n_p)
        return tr(wr), tr(wi)

    wf_r, wf_i = win_part(0, np.arange(L - 1, -1, -1))
    wb_r, wb_i = win_part(1, np.arange(L))
    win = jnp.concatenate([wf_r, wf_i, wb_r, wb_i], axis=-1)

    def wout_part(d, order):
        er = cp_r[order, d]
        ei = cp_i[order, d]
        tr = lambda w: w.transpose(1, 3, 0, 2).reshape(n_g, n_p, L * n_h)
        return tr(er), tr(-ei)

    of_r, of_i = wout_part(0, np.arange(1, L + 1))
    ob_r, ob_i = wout_part(1, np.arange(L, 0, -1))
    wout = jnp.concatenate([of_r, of_i, ob_r, ob_i], axis=1)

    glu = jnp.einsum('ts,ghk->gthsk', jnp.eye(L, dtype=F32), glu_w.astype(F32)).reshape(n_g, L * n_h, L * n_h)
    glub = jnp.tile(glu_b.astype(F32), (1, L)).reshape(n_g, 1, L * n_h)

    qr, qi = pw_r[L], pw_i[L]
    rows = []
    for _ in range(n_steps):
        rows.append((jnp.concatenate([qr, qr], -1), jnp.concatenate([-qi, qi], -1)))
        qr, qi = qr * qr - qi * qi, 2.0 * qr * qi
    coef = jnp.stack([jnp.stack([a1, a2], axis=0) for a1, a2 in rows], axis=0)
    coef = coef.transpose(3, 2, 0, 1, 4).reshape(n_g, 2 * n_steps * 2, 2 * n_p)
    return mt.astype(BF16), win.astype(BF16), wout.astype(BF16), glu.astype(BF16), glub, coef


def _s5_kernel(u_ref, perm_ref, mt_ref, win_ref, wout_ref, glu_ref, glub_ref, coef_ref, o_ref, *, n_steps):
    L = S5_CHUNK
    n_c = u_ref.shape[1] // L
    gpb, lh = mt_ref.shape[0], mt_ref.shape[1]
    ww = jnp.concatenate([u_ref.at[0][pl.ds(t, n_c, stride=L), :].astype(BF16) for t in range(L)], axis=1)
    xp = jnp.dot(ww, perm_ref[...], preferred_element_type=F32).astype(BF16)
    row = lax.broadcasted_iota(jnp.int32, (n_c, 2 * S5_STATE), 0)

    def scan(x, g, d):
        for k in range(n_steps):
            sh = 1 << k
            if d == 0:
                shifted = jnp.where(row >= sh, pltpu.roll(x, sh, 0), 0.0)
            else:
                shifted = jnp.where(row < n_c - sh, pltpu.roll(x, n_c - sh, 0), 0.0)
            base = (d * n_steps + k) * 2
            a1 = coef_ref[g, base:base + 1, :]
            a2 = coef_ref[g, base + 1:base + 2, :]
            x = x + a1 * shifted + a2 * pltpu.roll(shifted, S5_STATE, 1)
        if d == 0:
            return jnp.where(row >= 1, pltpu.roll(x, 1, 0), 0.0)
        return jnp.where(row < n_c - 1, pltpu.roll(x, n_c - 1, 0), 0.0)

    outs = []
    for g in range(gpb):
        u = xp[:, g * lh:(g + 1) * lh]
        st = jnp.dot(u, win_ref[g], preferred_element_type=F32)
        two_p = st.shape[1] // 2
        hcat = jnp.concatenate([scan(st[:, :two_p], g, 0), scan(st[:, two_p:], g, 1)], axis=1).astype(BF16)
        y = (jnp.dot(u, mt_ref[g], preferred_element_type=F32)
             + jnp.dot(hcat, wout_ref[g], preferred_element_type=F32))
        y = _gelu(y)
        gate = jax.nn.sigmoid(jnp.dot(y.astype(BF16), glu_ref[g], preferred_element_type=F32) + glub_ref[g])
        outs.append((y * gate).astype(BF16))
    back = lax.dot_general(jnp.concatenate(outs, axis=1), perm_ref[...], (((1,), (1,)), ((), ())),
                           preferred_element_type=F32)
    for t in range(L):
        o_ref.at[0][pl.ds(t, n_c, stride=L), :] = back[:, t * LANES:(t + 1) * LANES]


def _s5(ua, weights, n_steps):
    mt, win, wout, glu, glub, coef = weights
    bsz, s_len, width = ua.shape
    n_g = mt.shape[0]
    n_h = width // n_g
    L = S5_CHUNK
    gpb = LANES // n_h
    n_lb = width // LANES
    cols = L * LANES
    ci = np.arange(cols)
    t_i, g_i, h_i = ci // LANES, (ci % LANES) // n_h, ci % n_h
    perm = (jnp.asarray(g_i * (L * n_h) + t_i * n_h + h_i)[:, None] == jnp.arange(cols)[None, :]).astype(BF16)
    blk = lambda a: pl.BlockSpec((gpb,) + a.shape[1:], lambda b, lb: (lb, 0, 0))
    return pl.pallas_call(
        functools.partial(_s5_kernel, n_steps=n_steps),
        grid=(bsz, n_lb),
        in_specs=[pl.BlockSpec((1, s_len, LANES), lambda b, lb: (b, 0, lb)),
                  pl.BlockSpec((cols, cols), lambda b, lb: (0, 0)),
                  blk(mt), blk(win), blk(wout), blk(glu), blk(glub), blk(coef)],
        out_specs=pl.BlockSpec((1, s_len, LANES), lambda b, lb: (b, 0, lb)),
        out_shape=jax.ShapeDtypeStruct((bsz, s_len, width), F32),
        compiler_params=_cparams(("arbitrary", "arbitrary")),
        name="s5",
    )(ua, perm, mt, win, wout, glu, glub, coef)


def _attn_kernel(q_ref, k_ref, v_ref, o_ref, *, tk):
    tq = q_ref.shape[1]
    qs = jnp.concatenate([q_ref[0, :, h * HEAD_DIM:(h + 1) * HEAD_DIM] for h in range(GQA_GROUP)], axis=0)
    n_rows = qs.shape[0]
    n_kv = k_ref.shape[1] // tk

    def body(j, carry):
        m, l, acc = carry
        start = pl.multiple_of(j * tk, tk)
        kc = k_ref[0, pl.ds(start, tk), :]
        vc = v_ref[0, pl.ds(start, tk), :]
        s = lax.dot_general(qs, kc, (((1,), (1,)), ((), ())), preferred_element_type=F32)
        m_new = jnp.maximum(m, jnp.max(s, axis=-1, keepdims=True))
        p = jnp.exp(s - m_new)
        alpha = jnp.exp(m - m_new)
        l = alpha * l + jnp.sum(p, axis=-1, keepdims=True)
        acc = alpha * acc + jnp.dot(p.astype(BF16), vc, preferred_element_type=F32)
        return m_new, l, acc

    m0 = jnp.full((n_rows, 1), -jnp.inf, F32)
    l0 = jnp.zeros((n_rows, 1), F32)
    a0 = jnp.zeros((n_rows, HEAD_DIM), F32)
    _, l, acc = lax.fori_loop(0, n_kv, body, (m0, l0, a0), unroll=True)
    out = acc / l
    for h in range(GQA_GROUP):
        o_ref[0, :, h * HEAD_DIM:(h + 1) * HEAD_DIM] = out[h * tq:(h + 1) * tq].astype(BF16)


def _attention(q, k, v, tq=128, tk=2048):
    bsz, s_len, n_q = q.shape
    gw = GQA_GROUP * HEAD_DIM
    tk = min(tk, s_len)
    return pl.pallas_call(
        functools.partial(_attn_kernel, tk=tk),
        grid=(bsz, N_KV_HEADS, s_len // tq),
        in_specs=[pl.BlockSpec((1, tq, gw), lambda b, h, i: (b, i, h)),
                  pl.BlockSpec((1, s_len, HEAD_DIM), lambda b, h, i: (b, 0, h)),
                  pl.BlockSpec((1, s_len, HEAD_DIM), lambda b, h, i: (b, 0, h))],
        out_specs=pl.BlockSpec((1, tq, gw), lambda b, h, i: (b, i, h)),
        out_shape=jax.ShapeDtypeStruct((bsz, s_len, n_q), BF16),
        compiler_params=_cparams(("arbitrary", "arbitrary", "arbitrary")),
        name="attention",
    )(q, k, v)


def _store_peer_inputs(h2, ht_ref, h8_ref, rs_ref):
    h2t = h2.T
    ht_ref[...] = h2t.astype(BF16)
    amax = jnp.maximum(jnp.max(jnp.abs(h2t), axis=0, keepdims=True), FP8_TINY)
    h8_ref[...] = (h2t * (FP8_TARGET / amax)).astype(FP8)
    rs_ref[...] = amax * (1.0 / FP8_TARGET)


def _mid0_kernel(a_ref, b_ref, wa_ref, wb_ref, x_ref, g1_ref, g_ref, sc_ref, sh_ref, x1_ref, ht_ref, h8_ref, rs_ref):
    m = (jnp.dot(a_ref[0].astype(BF16), wa_ref[...], preferred_element_type=F32)
         + jnp.dot(b_ref[0], wb_ref[...], preferred_element_type=F32))
    x1 = x_ref[0] + g1_ref[0] * m
    x1_ref[0] = x1
    h2 = _ada_norm(x1, g_ref[...], sc_ref[0], sh_ref[0])
    _store_peer_inputs(h2, ht_ref, h8_ref, rs_ref)


def _mid0(a_out, b_out, w_a, w_b, x, g1, g, sc, sh, tm=256):
    bsz, s_len, d = x.shape
    n_t = s_len // tm
    wa_n, wb_n = w_a.shape[0], w_b.shape[0]
    vec = lambda: pl.BlockSpec((1, 1, d), lambda b, i: (b, 0, 0))
    return pl.pallas_call(
        _mid0_kernel,
        grid=(bsz, n_t),
        in_specs=[pl.BlockSpec((1, tm, wa_n), lambda b, i: (b, i, 0)),
                  pl.BlockSpec((1, tm, wb_n), lambda b, i: (b, i, 0)),
                  pl.BlockSpec((wa_n, d), lambda b, i: (0, 0)),
                  pl.BlockSpec((wb_n, d), lambda b, i: (0, 0)),
                  pl.BlockSpec((1, tm, d), lambda b, i: (b, i, 0)),
                  vec(),
                  pl.BlockSpec((1, d), lambda b, i: (0, 0)),
                  vec(), vec()],
        out_specs=[pl.BlockSpec((1, tm, d), lambda b, i: (b, i, 0)),
                   pl.BlockSpec((d, tm), lambda b, i: (0, b * n_t + i)),
                   pl.BlockSpec((d, tm), lambda b, i: (0, b * n_t + i)),
                   pl.BlockSpec((1, tm), lambda b, i: (0, b * n_t + i))],
        out_shape=[jax.ShapeDtypeStruct((bsz, s_len, d), F32),
                   jax.ShapeDtypeStruct((d, bsz * s_len), BF16),
                   jax.ShapeDtypeStruct((d, bsz * s_len), FP8),
                   jax.ShapeDtypeStruct((1, bsz * s_len), F32)],
        compiler_params=_cparams(("arbitrary", "arbitrary")),
        name="mid0",
    )(a_out, b_out, w_a, w_b, x, g1, g, sc, sh)


def _odd_kernel(xp_ref, pt_ref, g2p_ref, g_ref, sc_ref, sh_ref, wu_ref, wv_ref, lng_ref, ws_ref, bs_ref, wo_ref,
                g1_ref, g2n_ref, sc2_ref, sh2_ref, x1_ref, ht_ref, h8_ref, rs_ref, h_scr, acc_scr):
    gi = pl.program_id(2)
    n_groups = pl.num_programs(2)

    @pl.when(gi == 0)
    def _():
        x = xp_ref[0] + g2p_ref[0] * pt_ref[...].T
        x1_ref[0] = x
        h_scr[...] = _ada_norm(x, g_ref[...], sc_ref[0], sh_ref[0]).astype(BF16)
        acc_scr[...] = jnp.zeros_like(acc_scr)

    h = h_scr[...]
    zu = _gelu(jnp.dot(h, wu_ref[...], preferred_element_type=F32))
    zv = _gelu(jnp.dot(h, wv_ref[...], preferred_element_type=F32))
    mu = jnp.mean(zv, axis=-1, keepdims=True)
    cen = zv - mu
    var = jnp.mean(cen * cen, axis=-1, keepdims=True)
    vn = (cen * lax.rsqrt(var + EPS) * lng_ref[0]).astype(BF16)
    ws = ws_ref[0]
    bias = bs_ref[0]
    tm = zu.shape[0]
    gated = []
    for c in range(tm // SGU_CHUNK):
        sl = slice(c * SGU_CHUNK, (c + 1) * SGU_CHUNK)
        s = jnp.dot(ws, vn[sl], preferred_element_type=F32) + bias
        gated.append((zu[sl] * s).astype(BF16))
    gated = jnp.concatenate(gated, axis=0)
    acc_scr[...] += jnp.dot(gated, wo_ref[...], preferred_element_type=F32)

    @pl.when(gi == n_groups - 1)
    def _():
        x1 = x1_ref[0] + g1_ref[0] * acc_scr[...]
        x1_ref[0] = x1
        h2 = _ada_norm(x1, g2n_ref[...], sc2_ref[0], sh2_ref[0])
        _store_peer_inputs(h2, ht_ref, h8_ref, rs_ref)


def _odd(xp, peer_t, g2p, g, sc, sh, w_in, ln_g, w_s, b_s, w_out, g1, g2n, sc2, sh2, tm=512):
    bsz, s_len, d = xp.shape
    n_t = s_len // tm
    width = w_out.shape[0]
    gw = width // SGU_GROUPS
    vec = lambda: pl.BlockSpec((1, 1, d), lambda b, i, gi: (b, 0, 0))
    row = lambda: pl.BlockSpec((1, d), lambda b, i, gi: (0, 0))
    bias = jnp.broadcast_to(b_s.astype(F32)[:, :, None], (SGU_GROUPS, SGU_CHUNK, gw))
    return pl.pallas_call(
        _odd_kernel,
        grid=(bsz, n_t, SGU_GROUPS),
        in_specs=[pl.BlockSpec((1, tm, d), lambda b, i, gi: (b, i, 0)),
                  pl.BlockSpec((d, tm), lambda b, i, gi: (0, b * n_t + i)),
                  vec(),
                  row(), vec(), vec(),
                  pl.BlockSpec((d, gw), lambda b, i, gi: (0, gi)),
                  pl.BlockSpec((d, gw), lambda b, i, gi: (0, SGU_GROUPS + gi)),
                  pl.BlockSpec((1, 1, gw), lambda b, i, gi: (gi, 0, 0)),
                  pl.BlockSpec((1, SGU_CHUNK, SGU_CHUNK), lambda b, i, gi: (gi, 0, 0)),
                  pl.BlockSpec((1, SGU_CHUNK, gw), lambda b, i, gi: (gi, 0, 0)),
                  pl.BlockSpec((gw, d), lambda b, i, gi: (gi, 0)),
                  vec(), row(), vec(), vec()],
        out_specs=[pl.BlockSpec((1, tm, d), lambda b, i, gi: (b, i, 0)),
                   pl.BlockSpec((d, tm), lambda b, i, gi: (0, b * n_t + i)),
                   pl.BlockSpec((d, tm), lambda b, i, gi: (0, b * n_t + i)),
                   pl.BlockSpec((1, tm), lambda b, i, gi: (0, b * n_t + i))],
        out_shape=[jax.ShapeDtypeStruct((bsz, s_len, d), F32),
                   jax.ShapeDtypeStruct((d, bsz * s_len), BF16),
                   jax.ShapeDtypeStruct((d, bsz * s_len), FP8),
                   jax.ShapeDtypeStruct((1, bsz * s_len), F32)],
        scratch_shapes=[pltpu.VMEM((tm, d), BF16), pltpu.VMEM((tm, d), F32)],
        compiler_params=_cparams(("arbitrary", "arbitrary", "arbitrary")),
        name="odd_mixer",
    )(xp, peer_t, g2p, g, sc, sh, w_in, w_in, ln_g.reshape(SGU_GROUPS, 1, gw), w_s, bias, w_out, g1, g2n, sc2, sh2)


def _compose_kernel(k_ref, w_ref, o_ref):
    o_ref[0, 0] = lax.dot_general(k_ref[0], w_ref[...], (((1,), (1,)), ((), ())),
                                  preferred_element_type=F32, precision=HP)


def _compose_score_weights(w_q, k1, k2):
    d = w_q.shape[0]
    dk = k1.shape[1]
    keys = jnp.stack([k1, k2]).astype(F32)
    out = pl.pallas_call(
        _compose_kernel,
        grid=(2, PEER_HEADS),
        in_specs=[pl.BlockSpec((1, PEER_NKEYS, dk), lambda a, h: (a, 0, 0)),
                  pl.BlockSpec((d, dk), lambda a, h: (0, h * 2 + a))],
        out_specs=pl.BlockSpec((1, 1, PEER_NKEYS, d), lambda a, h: (a, h, 0, 0)),
        out_shape=jax.ShapeDtypeStruct((2, PEER_HEADS, PEER_NKEYS, d), F32),
        compiler_params=_cparams(("arbitrary", "arbitrary")),
        name="compose_scores",
    )(keys, w_q.astype(F32))
    return out.transpose(0, 2, 1, 3).reshape(2 * PEER_NKEYS * PEER_HEADS, d).astype(BF16)


def _batcher_pairs(n):
    pairs = []
    p = 1
    while p < n:
        k = p
        while k >= 1:
            for j in range(k % p, n - k, 2 * k):
                for i in range(min(k, n - j - k)):
                    if (i + j) // (2 * p) == (i + j + k) // (2 * p):
                        pairs.append((i + j, i + j + k))
            k //= 2
        p *= 2
    return pairs


_SORT16 = _batcher_pairs(PEER_TOPK)


def _vmax(a, b):
    if a is None:
        return b
    if b is None:
        return a
    return jnp.maximum(a, b)


def _vmin(a, b):
    if a is None or b is None:
        return None
    return jnp.minimum(a, b)


def _sort16_desc(v):
    v = list(v)
    for i, j in _SORT16:
        v[i], v[j] = _vmax(v[i], v[j]), _vmin(v[i], v[j])
    return v


def _merge_top16(a, b):
    n = PEER_TOPK
    c = [_vmax(a[i], b[n - 1 - i]) for i in range(n)]
    d = n // 2
    while d >= 1:
        for i in range(n):
            if (i & d) == 0:
                c[i], c[i + d] = _vmax(c[i], c[i + d]), _vmin(c[i], c[i + d])
        d //= 2
    return c


def _top16_sorted(vals):
    groups = [_sort16_desc(vals[i:i + PEER_TOPK]) for i in range(0, len(vals), PEER_TOPK)]
    while len(groups) > 1:
        groups = [_merge_top16(groups[i], groups[i + 1]) for i in range(0, len(groups), 2)]
    return groups[0]


def _route_kernel(ht_ref, w_ref, cnt_ref, p1_ref, r2_ref, p2_ref, r2_scr, p2_scr, *, tw):
    for sp in range(ht_ref.shape[1] // tw):
        _route_block(ht_ref, w_ref, cnt_ref, p1_ref, r2_ref, p2_ref, r2_scr, p2_scr, sp * tw, tw)


def _route_block(ht_ref, w_ref, cnt_ref, p1_ref, r2_ref, p2_ref, r2_scr, p2_scr, lane0, tw):
    nh, nk = PEER_HEADS, PEER_NKEYS
    tok = slice(lane0, lane0 + tw)
    lb0 = lane0 // LANES
    st = jnp.dot(w_ref[...], ht_ref[:, tok], preferred_element_type=F32)
    s1 = [st[i * nh:(i + 1) * nh] for i in range(nk)]
    s2 = [st[(nk + i) * nh:(nk + i + 1) * nh] for i in range(nk)]
    a = _top16_sorted(s1)
    b = _top16_sorted(s2)
    cands = [[a[k] + b[l] for l in range(PEER_TOPK // (k + 1))] for k in range(PEER_TOPK)]
    flat = [c for rowc in cands[1:] for c in rowc]
    flat = flat + [None] * (-len(flat) % PEER_TOPK)
    groups = [cands[0]] + [_sort16_desc(flat[i:i + PEER_TOPK]) for i in range(0, len(flat), PEER_TOPK)]
    top = groups[0]
    for gr in groups[1:]:
        top = _merge_top16(top, gr)
    theta = top[PEER_TOPK - 1]
    z = jnp.ones_like(theta)
    for t in top[1:]:
        z = z + jnp.exp(t - top[0])
    inv_z = 1.0 / z
    n_lb = tw // LANES
    cnt_k = []
    for k in range(PEER_TOPK):
        c = jnp.zeros_like(theta)
        for cand in cands[k]:
            c = c + jnp.where(cand >= theta, 1.0, 0.0)
        cnt_k.append(c)
    for i in range(nk):
        rows = slice(i * nh, (i + 1) * nh)
        cnt = jnp.zeros_like(theta)
        rank = jnp.full_like(theta, float(PEER_TOPK))
        for l in range(PEER_TOPK - 1, -1, -1):
            cnt = jnp.where(s1[i] >= a[l], cnt_k[l], cnt)
            rank = jnp.where(s2[i] >= b[l], float(l), rank)
        cnt_ref[rows, tok] = cnt
        p1_ref[rows, tok] = jnp.exp(s1[i] - a[0]) * inv_z
        p2 = jnp.exp(s2[i] - b[0])
        for lb in range(n_lb):
            lanes = slice(lb * LANES, (lb + 1) * LANES)
            r2_scr[lb0 + lb, rows, :] = rank[:, lanes]
            p2_scr[lb0 + lb, rows, :] = p2[:, lanes]
    for h in range(nh):
        for lb in range(lb0, lb0 + n_lb):
            lanes = slice(lb * LANES, (lb + 1) * LANES)
            r2_ref[h, :, lanes] = r2_scr.at[lb][pl.ds(h, nk, stride=nh), :].astype(BF16)
            p2_ref[h, :, lanes] = p2_scr.at[lb][pl.ds(h, nk, stride=nh), :].astype(BF16)


def _route(ht, w_s, tm=512, tw=256):
    d, t = ht.shape
    nh, nk = PEER_HEADS, PEER_NKEYS
    rows = nh * nk
    tw = min(tw, tm)
    return pl.pallas_call(
        functools.partial(_route_kernel, tw=tw),
        grid=(t // tm,),
        in_specs=[pl.BlockSpec((d, tm), lambda i: (0, i)),
                  pl.BlockSpec((2 * rows, d), lambda i: (0, 0))],
        out_specs=[pl.BlockSpec((rows, tm), lambda i: (0, i)),
                   pl.BlockSpec((rows, tm), lambda i: (0, i)),
                   pl.BlockSpec((nh, nk, tm), lambda i: (0, 0, i)),
                   pl.BlockSpec((nh, nk, tm), lambda i: (0, 0, i))],
        out_shape=[jax.ShapeDtypeStruct((rows, t), F32),
                   jax.ShapeDtypeStruct((rows, t), F32),
                   jax.ShapeDtypeStruct((nh, nk, t), BF16),
                   jax.ShapeDtypeStruct((nh, nk, t), BF16)],
        scratch_shapes=[pltpu.VMEM((tm // LANES, rows, LANES), F32), pltpu.VMEM((tm // LANES, rows, LANES), F32)],
        compiler_params=_cparams(("arbitrary",)),
        name="peer_route",
    )(ht, w_s)


def _peer_kernel(h8_ref, rs_ref, u_ref, vt_ref, cnt_ref, p1_ref, r2_ref, p2_ref, o_ref, *, n_split):
    ei = pl.program_id(1)
    nh, nk = PEER_HEADS, PEER_NKEYS
    te = u_ref.shape[0]
    ts = te // n_split

    @pl.when(ei == 0)
    def _():
        o_ref[...] = jnp.zeros_like(o_ref)

    gates = []
    for ii in range(te // nk):
        key_i = ei * (te // nk) + ii
        gate = None
        for h in range(nh):
            r = key_i * nh + h
            cnt = cnt_ref[pl.ds(r, 1), :].astype(BF16)
            p1 = p1_ref[pl.ds(r, 1), :].astype(BF16)
            p2 = p2_ref[h]
            term = jnp.where(r2_ref[h] < cnt, p2, jnp.zeros_like(p2)) * p1
            gate = term if gate is None else gate + term
        gates.append(gate)

    wts = []
    for sp in range(n_split):
        act = jnp.dot(u_ref[sp * ts:(sp + 1) * ts, :], h8_ref[...], preferred_element_type=F32) * rs_ref[...]
        gate = jnp.concatenate(gates[sp * (ts // nk):(sp + 1) * (ts // nk)], axis=0)
        wts.append(_gelu(act.astype(BF16)) * gate)
    wt = jnp.concatenate(wts, axis=0)
    o_ref[...] += jnp.dot(vt_ref[...], wt, preferred_element_type=F32)


def _peer(h8, rs, u_tabs, vt_tabs, layer, cnt, p1, r2, p2, tm=512, te=1024, n_split=8):
    d, t = h8.shape
    n_e = u_tabs.shape[1]
    nh, nk = PEER_HEADS, PEER_NKEYS
    rows = nh * nk
    return pl.pallas_call(
        functools.partial(_peer_kernel, n_split=n_split),
        grid=(t // tm, n_e // te),
        in_specs=[pl.BlockSpec((d, tm), lambda i, e: (0, i)),
                  pl.BlockSpec((1, tm), lambda i, e: (0, i)),
                  pl.BlockSpec((None, te, d), lambda i, e: (layer, e, 0)),
                  pl.BlockSpec((None, d, te), lambda i, e: (layer, 0, e)),
                  pl.BlockSpec((rows, tm), lambda i, e: (0, i)),
                  pl.BlockSpec((rows, tm), lambda i, e: (0, i)),
                  pl.BlockSpec((nh, nk, tm), lambda i, e: (0, 0, i)),
                  pl.BlockSpec((nh, nk, tm), lambda i, e: (0, 0, i))],
        out_specs=pl.BlockSpec((d, tm), lambda i, e: (0, i)),
        out_shape=jax.ShapeDtypeStruct((d, t), F32),
        compiler_params=_cparams(("arbitrary", "arbitrary")),
        name="peer_dense",
    )(h8, rs, u_tabs, vt_tabs, cnt, p1, r2, p2)


def _resid_kernel(x_ref, pt_ref, g2_ref, fg_ref, o_ref, *, final_norm):
    x2 = x_ref[0] + g2_ref[0] * pt_ref[...].T
    if final_norm:
        ms = jnp.mean(x2 * x2, axis=-1, keepdims=True)
        x2 = x2 * lax.rsqrt(ms + EPS) * fg_ref[...]
    o_ref[0] = x2


def _resid(x1, peer_t, g2, final_g, final_norm, tm=256):
    bsz, s_len, d = x1.shape
    n_t = s_len // tm
    return pl.pallas_call(
        functools.partial(_resid_kernel, final_norm=final_norm),
        grid=(bsz, n_t),
        in_specs=[pl.BlockSpec((1, tm, d), lambda b, i: (b, i, 0)),
                  pl.BlockSpec((d, tm), lambda b, i: (0, b * n_t + i)),
                  pl.BlockSpec((1, 1, d), lambda b, i: (b, 0, 0)),
                  pl.BlockSpec((1, d), lambda b, i: (0, 0))],
        out_specs=pl.BlockSpec((1, tm, d), lambda b, i: (b, i, 0)),
        out_shape=jax.ShapeDtypeStruct((bsz, s_len, d), F32),
        compiler_params=_cparams(("arbitrary", "arbitrary")),
        name="peer_residual",
    )(x1, peer_t, g2, final_g)


def _peer_layer(ht, h8, rs, w_q, k1, k2, u_tabs, u_unscale, vt_tabs, layer):
    w_s = _compose_score_weights(w_q, k1, k2)
    cnt, p1, r2, p2 = _route(ht, w_s)
    return _peer(h8, rs * u_unscale[layer], u_tabs, vt_tabs, layer, cnt, p1, r2, p2)


def _rope_tables(s_len):
    rows = s_len // GRID_W
    row = jnp.repeat(jnp.arange(rows, dtype=F32), GRID_W)
    col = jnp.tile(jnp.arange(GRID_W, dtype=F32), rows)
    n_pair_axis = HEAD_DIM // 4
    inv_freq = ROPE_THETA ** (-jnp.arange(n_pair_axis, dtype=F32) / n_pair_axis)
    ang = jnp.concatenate([row[:, None] * inv_freq, col[:, None] * inv_freq], axis=-1)
    cos, sin = jnp.cos(ang), jnp.sin(ang)
    return jnp.concatenate([cos, cos], axis=-1), jnp.concatenate([-sin, sin], axis=-1)


def _deinterleave_perm():
    half = HEAD_DIM // 2
    return np.concatenate([2 * np.arange(half), 2 * np.arange(half) + 1])


def kernel(x, c, ada_w, ada_b, norm_g, even_w_in, even_w_out, s5_lam_re, s5_lam_im, s5_log_dt, s5_b_re, s5_b_im, s5_c_re, s5_c_im, s5_d, s5_glu_w, s5_glu_b, q_norm_g, k_norm_g, odd_w_in, odd_w_out, sgu_ln_g, sgu_w, sgu_b, peer_w_q, peer_k1, peer_k2, peer_u, peer_v, final_g):
    bsz, s_len, d = x.shape
    depth = ada_w.shape[0]
    mod = _modulation(c, ada_w, ada_b)
    fg = final_g.reshape(1, d)
    n_a = s5_lam_re.shape[2] * S5_GROUP
    n_q = N_Q_HEADS * HEAD_DIM
    n_kv = N_KV_HEADS * HEAD_DIM
    n_steps = max(1, int(math.log2(s_len // S5_CHUNK)))
    cos_t, sin_t = _rope_tables(s_len)
    perm = _deinterleave_perm()
    u_amax = jnp.maximum(jnp.max(jnp.abs(peer_u), axis=(1, 2)), FP8_TINY)
    u_tabs = (peer_u * (FP8_TARGET / u_amax)[:, None, None]).astype(FP8)
    u_unscale = u_amax * (1.0 / FP8_TARGET)
    vt_tabs = jnp.swapaxes(peer_v, 1, 2).astype(BF16)

    pending = None
    for layer in range(depth):
        sh1, sc1, g1, sh2, sc2, g2 = [m.reshape(bsz, 1, d) for m in jnp.split(mod[layer], 6, axis=-1)]
        ng1 = norm_g[layer, 0].reshape(1, d)
        ng2 = norm_g[layer, 1].reshape(1, d)
        i = layer // 2
        if layer % 2 == 0:
            if pending is not None:
                x = _resid(*pending, fg, False)
            w_in = even_w_in[i]
            col = np.arange(w_in.shape[1])
            for hd in range(N_Q_HEADS + N_KV_HEADS):
                lo = n_a + hd * HEAD_DIM
                col[lo:lo + HEAD_DIM] = lo + perm
            w_in_p = w_in[:, col].astype(BF16)
            gq = (q_norm_g[i][perm] * (HEAD_DIM ** -0.5)).reshape(1, HEAD_DIM)
            gk = k_norm_g[i][perm].reshape(1, HEAD_DIM)
            ua, q, k, v = _front0(x, ng1, sc1, sh1, w_in_p, gq, gk, cos_t, sin_t, n_a, n_q, n_kv)
            weights = _s5_weights(s5_lam_re[i], s5_lam_im[i], s5_log_dt[i], s5_b_re[i], s5_b_im[i],
                                  s5_c_re[i], s5_c_im[i], s5_d[i], s5_glu_w[i], s5_glu_b[i], n_steps)
            a_out = _s5(ua, weights, n_steps)
            b_out = _attention(q, k, v)
            w_out = even_w_out[i].astype(BF16)
            x1, ht, h8, rs = _mid0(a_out, b_out, w_out[:n_a], w_out[n_a:], x, g1, ng2, sc2, sh2)
        else:
            x1, ht, h8, rs = _odd(*pending, ng1, sc1, sh1, odd_w_in[i].astype(BF16), sgu_ln_g[i],
                                  sgu_w[i].astype(BF16), sgu_b[i], odd_w_out[i].astype(BF16), g1, ng2, sc2, sh2)
        peer_t = _peer_layer(ht, h8, rs, peer_w_q[layer], peer_k1[layer], peer_k2[layer],
                             u_tabs, u_unscale, vt_tabs, layer)
        pending = (x1, peer_t, g2)
    return _resid(*pending, fg, True)
```

```python
import functools
import math

import numpy as np
import jax
import jax.numpy as jnp
from jax import lax
from jax.experimental import pallas as pl
from jax.experimental.pallas import tpu as pltpu

F32 = jnp.float32
BF16 = jnp.bfloat16
FP8 = jnp.float8_e4m3fn
FP8_TARGET = 224.0
FP8_TINY = 1e-30
EPS = 1e-6
HP = lax.Precision.HIGHEST

V7X_VMEM_LIMIT_BYTES = 56 * 1024 * 1024
LANES = 128
SUBLANES = 8

GRID_W = 64
S5_GROUP = 16
S5_STATE = 64
S5_CHUNK = 16
HEAD_DIM = 128
N_Q_HEADS = 8
N_KV_HEADS = 2
GQA_GROUP = N_Q_HEADS // N_KV_HEADS
ROPE_THETA = 10000.0
SGU_CHUNK = 128
SGU_GROUPS = 8
PEER_HEADS = 8
PEER_NKEYS = 128
PEER_TOPK = 16
DT_CLAMP = -1e-4


def _cparams(semantics):
    return pltpu.CompilerParams(dimension_semantics=semantics, vmem_limit_bytes=V7X_VMEM_LIMIT_BYTES)


def _gelu(x):
    return jax.nn.gelu(x)


def _mod_kernel(c_ref, w_ref, b_ref, o_ref):
    c = c_ref[...]
    ca = c * jax.nn.sigmoid(c)
    o_ref[0] = jnp.dot(ca, w_ref[0], preferred_element_type=F32, precision=HP) + b_ref[0]


def _modulation(c, ada_w, ada_b):
    depth, d, n = ada_w.shape
    bsz = c.shape[0]
    cpad = jnp.zeros((SUBLANES, d), F32).at[:bsz].set(c)
    tn = 1024
    out = pl.pallas_call(
        _mod_kernel,
        grid=(depth, n // tn),
        in_specs=[pl.BlockSpec((SUBLANES, d), lambda l, j: (0, 0)),
                  pl.BlockSpec((1, d, tn), lambda l, j: (l, 0, j)),
                  pl.BlockSpec((1, 1, tn), lambda l, j: (l, 0, j))],
        out_specs=pl.BlockSpec((1, SUBLANES, tn), lambda l, j: (l, 0, j)),
        out_shape=jax.ShapeDtypeStruct((depth, SUBLANES, n), F32),
        compiler_params=_cparams(("arbitrary", "arbitrary")),
        name="modulation",
    )(cpad, ada_w, ada_b.reshape(depth, 1, n))
    return out[:, :bsz]


def _ada_norm(x, g, sc, sh):
    ms = jnp.mean(x * x, axis=-1, keepdims=True)
    return x * lax.rsqrt(ms + EPS) * (g * (1.0 + sc)) + sh


def _front0_kernel(x_ref, g_ref, sc_ref, sh_ref, w_ref, gq_ref, gk_ref, cos_ref, sin_ref,
                   ua_ref, q_ref, k_ref, v_ref, *, n_a, n_q, n_kv):
    h = _ada_norm(x_ref[0], g_ref[...], sc_ref[0], sh_ref[0])
    z = jnp.dot(h.astype(BF16), w_ref[...], preferred_element_type=F32)
    ua_ref[0] = z[:, :n_a]
    cos = cos_ref[...]
    sin = sin_ref[...]

    def norm_rope(y, g):
        y = y * lax.rsqrt(jnp.mean(y * y, axis=-1, keepdims=True) + EPS) * g
        return y * cos + pltpu.roll(y, HEAD_DIM // 2, 1) * sin

    for hd in range(n_q // HEAD_DIM):
        lo = n_a + hd * HEAD_DIM
        q_ref[0, :, hd * HEAD_DIM:(hd + 1) * HEAD_DIM] = norm_rope(z[:, lo:lo + HEAD_DIM], gq_ref[...]).astype(BF16)
    for hd in range(n_kv // HEAD_DIM):
        lo = n_a + n_q + hd * HEAD_DIM
        k_ref[0, :, hd * HEAD_DIM:(hd + 1) * HEAD_DIM] = norm_rope(z[:, lo:lo + HEAD_DIM], gk_ref[...]).astype(BF16)
    v_ref[0] = z[:, n_a + n_q + n_kv:].astype(BF16)


def _front0(x, g, sc, sh, w_in, gq, gk, cos_t, sin_t, n_a, n_q, n_kv, tm=256):
    bsz, s_len, d = x.shape
    n_all = w_in.shape[1]
    vec = lambda: pl.BlockSpec((1, 1, d), lambda b, i: (b, 0, 0))
    return pl.pallas_call(
        functools.partial(_front0_kernel, n_a=n_a, n_q=n_q, n_kv=n_kv),
        grid=(bsz, s_len // tm),
        in_specs=[pl.BlockSpec((1, tm, d), lambda b, i: (b, i, 0)),
                  pl.BlockSpec((1, d), lambda b, i: (0, 0)),
                  vec(), vec(),
                  pl.BlockSpec((d, n_all), lambda b, i: (0, 0)),
                  pl.BlockSpec((1, HEAD_DIM), lambda b, i: (0, 0)),
                  pl.BlockSpec((1, HEAD_DIM), lambda b, i: (0, 0)),
                  pl.BlockSpec((tm, HEAD_DIM), lambda b, i: (i, 0)),
                  pl.BlockSpec((tm, HEAD_DIM), lambda b, i: (i, 0))],
        out_specs=[pl.BlockSpec((1, tm, n_a), lambda b, i: (b, i, 0)),
                   pl.BlockSpec((1, tm, n_q), lambda b, i: (b, i, 0)),
                   pl.BlockSpec((1, tm, n_kv), lambda b, i: (b, i, 0)),
                   pl.BlockSpec((1, tm, n_kv), lambda b, i: (b, i, 0))],
        out_shape=[jax.ShapeDtypeStruct((bsz, s_len, n_a), F32),
                   jax.ShapeDtypeStruct((bsz, s_len, n_q), BF16),
                   jax.ShapeDtypeStruct((bsz, s_len, n_kv), BF16),
                   jax.ShapeDtypeStruct((bsz, s_len, n_kv), BF16)],
        compiler_params=_cparams(("arbitrary", "arbitrary")),
        name="front0",
    )(x, g, sc, sh, w_in, gq, gk, cos_t, sin_t)


def _taps_kernel(cr_ref, ci_ref, br_ref, bi_ref, o_ref):
    for j in range(cr_ref.shape[0]):
        o_ref[j] = (jnp.dot(cr_ref[j], br_ref[j], preferred_element_type=F32, precision=HP)
                    - jnp.dot(ci_ref[j], bi_ref[j], preferred_element_type=F32, precision=HP))


def _s5_taps(cp_r, cp_i, bb_r, bb_i, per_step=8):
    n_b, rows, n_p = cp_r.shape
    n_h = bb_r.shape[-1]
    per_step = min(per_step, n_b)
    lhs = pl.BlockSpec((per_step, rows, n_p), lambda i: (i, 0, 0))
    rhs = pl.BlockSpec((per_step, n_p, n_h), lambda i: (i, 0, 0))
    return pl.pallas_call(
        _taps_kernel,
        grid=(n_b // per_step,),
        in_specs=[lhs, lhs, rhs, rhs],
        out_specs=pl.BlockSpec((per_step, rows, n_h), lambda i: (i, 0, 0)),
        out_shape=jax.ShapeDtypeStruct((n_b, rows, n_h), F32),
        compiler_params=_cparams(("arbitrary",)),
        name="s5_taps",
    )(cp_r, cp_i, bb_r, bb_i)


def _s5_weights(lam_re, lam_im, log_dt, b_re, b_im, c_re, c_im, d_skip, glu_w, glu_b, n_steps):
    L = S5_CHUNK
    n_g, n_p = lam_re.shape[1], lam_re.shape[2]
    n_h = b_re.shape[-1]
    lr = jnp.minimum(lam_re.astype(F32), DT_CLAMP)
    li = lam_im.astype(F32)
    dt = jnp.exp(log_dt.astype(F32))[..., None]
    mag = jnp.exp(lr * dt)
    ar = mag * jnp.cos(li * dt)
    ai = mag * jnp.sin(li * dt)
    den = lr * lr + li * li
    nr = ar - 1.0
    cr = (nr * lr + ai * li) / den
    ci = (ai * lr - nr * li) / den
    bbr = cr[..., None] * b_re - ci[..., None] * b_im
    bbi = cr[..., None] * b_im + ci[..., None] * b_re
    pr, pi = [jnp.ones_like(ar)], [jnp.zeros_like(ai)]
    for _ in range(L):
        pr_n = pr[-1] * ar - pi[-1] * ai
        pi_n = pr[-1] * ai + pi[-1] * ar
        pr.append(pr_n)
        pi.append(pi_n)
    pw_r = jnp.stack(pr)
    pw_i = jnp.stack(pi)
    cre = c_re.astype(F32)[None]
    cim = c_im.astype(F32)[None]
    cp_r = cre * pw_r[:, :, :, None, :] - cim * pw_i[:, :, :, None, :]
    cp_i = cre * pw_i[:, :, :, None, :] + cim * pw_r[:, :, :, None, :]
    to_rows = lambda c: c[:L].transpose(1, 2, 0, 3, 4).reshape(2 * n_g, L * n_h, n_p)
    taps = _s5_taps(to_rows(cp_r), to_rows(cp_i), bbr.reshape(2 * n_g, n_p, n_h), bbi.reshape(2 * n_g, n_p, n_h))
    kern = taps.reshape(2, n_g, L, n_h, n_h).transpose(2, 0, 1, 3, 4)
    kf, kb = kern[:, 0], kern[:, 1]
    centre = kf[0] + kb[0] + d_skip.astype(F32)[:, :, None] * jnp.eye(n_h, dtype=F32)
    ktab = jnp.concatenate([kb[1:][::-1], centre[None], kf[1:]], axis=0)
    idx = np.arange(L)[:, None] - np.arange(L)[None, :] + L - 1
    m_full = ktab[idx]
    mt = m_full.transpose(2, 1, 4, 0, 3).reshape(n_g, L * n_h, L * n_h)

    def win_part(d, order):
        wr = pw_r[order, d][:, :, :, None] * bbr[d][None] - pw_i[order, d][:, :, :, None] * bbi[d][None]
        wi = pw_r[order, d][:, :, :, None] * bbi[d][None] + pw_i[order, d][:, :, :, None] * bbr[d][None]
        tr = lambda w: w.transpose(1, 0, 3, 2).reshape(n_g, L * n_h, n_p)
        return tr(wr), tr(wi)

    wf_r, wf_i = win_part(0, np.arange(L - 1, -1, -1))
    wb_r, wb_i = win_part(1, np.arange(L))
    win = jnp.concatenate([wf_r, wf_i, wb_r, wb_i], axis=-1)

    def wout_part(d, order):
        er = cp_r[order, d]
        ei = cp_i[order, d]
        tr = lambda w: w.transpose(1, 3, 0, 2).reshape(n_g, n_p, L * n_h)
        return tr(er), tr(-ei)

    of_r, of_i = wout_part(0, np.arange(1, L + 1))
    ob_r, ob_i = wout_part(1, np.arange(L, 0, -1))
    wout = jnp.concatenate([of_r, of_i, ob_r, ob_i], axis=1)

    glu = jnp.einsum('ts,ghk->gthsk', jnp.eye(L, dtype=F32), glu_w.astype(F32)).reshape(n_g, L * n_h, L * n_h)
    glub = jnp.tile(glu_b.astype(F32), (1, L)).reshape(n_g, 1, L * n_h)

    qr, qi = pw_r[L], pw_i[L]
    rows = []
    for _ in range(n_steps):
        rows.append((jnp.concatenate([qr, qr], -1), jnp.concatenate([-qi, qi], -1)))
        qr, qi = qr * qr - qi * qi, 2.0 * qr * qi
    coef = jnp.stack([jnp.stack([a1, a2], axis=0) for a1, a2 in rows], axis=0)
    coef = coef.transpose(3, 2, 0, 1, 4).reshape(n_g, 2 * n_steps * 2, 2 * n_p)
    return mt.astype(BF16), win.astype(BF16), wout.astype(BF16), glu.astype(BF16), glub, coef


def _s5_kernel(u_ref, perm_ref, mt_ref, win_ref, wout_ref, glu_ref, glub_ref, coef_ref, o_ref, *, n_steps):
    L = S5_CHUNK
    n_c = u_ref.shape[1] // L
    gpb, lh = mt_ref.shape[0], mt_ref.shape[1]
    ww = jnp.concatenate([u_ref.at[0][pl.ds(t, n_c, stride=L), :].astype(BF16) for t in range(L)], axis=1)
    xp = jnp.dot(ww, perm_ref[...], preferred_element_type=F32).astype(BF16)
    row = lax.broadcasted_iota(jnp.int32, (n_c, 2 * S5_STATE), 0)

    def scan(x, g, d):
        for k in range(n_steps):
            sh = 1 << k
            if d == 0:
                shifted = jnp.where(row >= sh, pltpu.roll(x, sh, 0), 0.0)
            else:
                shifted = jnp.where(row < n_c - sh, pltpu.roll(x, n_c - sh, 0), 0.0)
            base = (d * n_steps + k) * 2
            a1 = coef_ref[g, base:base + 1, :]
            a2 = coef_ref[g, base + 1:base + 2, :]
            x = x + a1 * shifted + a2 * pltpu.roll(shifted, S5_STATE, 1)
        if d == 0:
            return jnp.where(row >= 1, pltpu.roll(x, 1, 0), 0.0)
        return jnp.where(row < n_c - 1, pltpu.roll(x, n_c - 1, 0), 0.0)

    outs = []
    for g in range(gpb):
        u = xp[:, g * lh:(g + 1) * lh]
        st = jnp.dot(u, win_ref[g], preferred_element_type=F32)
        two_p = st.shape[1] // 2
        hcat = jnp.concatenate([scan(st[:, :two_p], g, 0), scan(st[:, two_p:], g, 1)], axis=1).astype(BF16)
        y = (jnp.dot(u, mt_ref[g], preferred_element_type=F32)
             + jnp.dot(hcat, wout_ref[g], preferred_element_type=F32))
        y = _gelu(y)
        gate = jax.nn.sigmoid(jnp.dot(y.astype(BF16), glu_ref[g], preferred_element_type=F32) + glub_ref[g])
        outs.append((y * gate).astype(BF16))
    back = lax.dot_general(jnp.concatenate(outs, axis=1), perm_ref[...], (((1,), (1,)), ((), ())),
                           preferred_element_type=F32)
    for t in range(L):
        o_ref.at[0][pl.ds(t, n_c, stride=L), :] = back[:, t * LANES:(t + 1) * LANES]


def _s5(ua, weights, n_steps):
    mt, win, wout, glu, glub, coef = weights
    bsz, s_len, width = ua.shape
    n_g = mt.shape[0]
    n_h = width // n_g
    L = S5_CHUNK
    gpb = LANES // n_h
    n_lb = width // LANES
    cols = L * LANES
    ci = np.arange(cols)
    t_i, g_i, h_i = ci // LANES, (ci % LANES) // n_h, ci % n_h
    perm = (jnp.asarray(g_i * (L * n_h) + t_i * n_h + h_i)[:, None] == jnp.arange(cols)[None, :]).astype(BF16)
    blk = lambda a: pl.BlockSpec((gpb,) + a.shape[1:], lambda b, lb: (lb, 0, 0))
    return pl.pallas_call(
        functools.partial(_s5_kernel, n_steps=n_steps),
        grid=(bsz, n_lb),
        in_specs=[pl.BlockSpec((1, s_len, LANES), lambda b, lb: (b, 0, lb)),
                  pl.BlockSpec((cols, cols), lambda b, lb: (0, 0)),
                  blk(mt), blk(win), blk(wout), blk(glu), blk(glub), blk(coef)],
        out_specs=pl.BlockSpec((1, s_len, LANES), lambda b, lb: (b, 0, lb)),
        out_shape=jax.ShapeDtypeStruct((bsz, s_len, width), F32),
        compiler_params=_cparams(("arbitrary", "arbitrary")),
        name="s5",
    )(ua, perm, mt, win, wout, glu, glub, coef)


def _attn_kernel(q_ref, k_ref, v_ref, o_ref, *, tk):
    tq = q_ref.shape[1]
    qs = jnp.concatenate([q_ref[0, :, h * HEAD_DIM:(h + 1) * HEAD_DIM] for h in range(GQA_GROUP)], axis=0)
    n_rows = qs.shape[0]
    n_kv = k_ref.shape[1] // tk

    def body(j, carry):
        m, l, acc = carry
        start = pl.multiple_of(j * tk, tk)
        kc = k_ref[0, pl.ds(start, tk), :]
        vc = v_ref[0, pl.ds(start, tk), :]
        s = lax.dot_general(qs, kc, (((1,), (1,)), ((), ())), preferred_element_type=F32)
        m_new = jnp.maximum(m, jnp.max(s, axis=-1, keepdims=True))
        p = jnp.exp(s - m_new)
        alpha = jnp.exp(m - m_new)
        l = alpha * l + jnp.sum(p, axis=-1, keepdims=True)
        acc = alpha * acc + jnp.dot(p.astype(BF16), vc, preferred_element_type=F32)
        return m_new, l, acc

    m0 = jnp.full((n_rows, 1), -jnp.inf, F32)
    l0 = jnp.zeros((n_rows, 1), F32)
    a0 = jnp.zeros((n_rows, HEAD_DIM), F32)
    _, l, acc = lax.fori_loop(0, n_kv, body, (m0, l0, a0), unroll=True)
    out = acc / l
    for h in range(GQA_GROUP):
        o_ref[0, :, h * HEAD_DIM:(h + 1) * HEAD_DIM] = out[h * tq:(h + 1) * tq].astype(BF16)


def _attention(q, k, v, tq=128, tk=2048):
    bsz, s_len, n_q = q.shape
    gw = GQA_GROUP * HEAD_DIM
    tk = min(tk, s_len)
    return pl.pallas_call(
        functools.partial(_attn_kernel, tk=tk),
        grid=(bsz, N_KV_HEADS, s_len // tq),
        in_specs=[pl.BlockSpec((1, tq, gw), lambda b, h, i: (b, i, h)),
                  pl.BlockSpec((1, s_len, HEAD_DIM), lambda b, h, i: (b, 0, h)),
                  pl.BlockSpec((1, s_len, HEAD_DIM), lambda b, h, i: (b, 0, h))],
        out_specs=pl.BlockSpec((1, tq, gw), lambda b, h, i: (b, i, h)),
        out_shape=jax.ShapeDtypeStruct((bsz, s_len, n_q), BF16),
        compiler_params=_cparams(("arbitrary", "arbitrary", "arbitrary")),
        name="attention",
    )(q, k, v)


def _store_peer_inputs(h2, ht_ref, h8_ref, rs_ref):
    h2t = h2.T
    ht_ref[...] = h2t.astype(BF16)
    amax = jnp.maximum(jnp.max(jnp.abs(h2t), axis=0, keepdims=True), FP8_TINY)
    h8_ref[...] = (h2t * (FP8_TARGET / amax)).astype(FP8)
    rs_ref[0:1, :] = amax * (1.0 / FP8_TARGET)
    rs_ref[1:2, :] = jnp.sqrt(jnp.sum(h2t * h2t, axis=0, keepdims=True))


def _mid0_kernel(a_ref, b_ref, wa_ref, wb_ref, x_ref, g1_ref, g_ref, sc_ref, sh_ref, x1_ref, ht_ref, h8_ref, rs_ref):
    m = (jnp.dot(a_ref[0].astype(BF16), wa_ref[...], preferred_element_type=F32)
         + jnp.dot(b_ref[0], wb_ref[...], preferred_element_type=F32))
    x1 = x_ref[0] + g1_ref[0] * m
    x1_ref[0] = x1
    h2 = _ada_norm(x1, g_ref[...], sc_ref[0], sh_ref[0])
    _store_peer_inputs(h2, ht_ref, h8_ref, rs_ref)


def _mid0(a_out, b_out, w_a, w_b, x, g1, g, sc, sh, tm=256):
    bsz, s_len, d = x.shape
    n_t = s_len // tm
    wa_n, wb_n = w_a.shape[0], w_b.shape[0]
    vec = lambda: pl.BlockSpec((1, 1, d), lambda b, i: (b, 0, 0))
    return pl.pallas_call(
        _mid0_kernel,
        grid=(bsz, n_t),
        in_specs=[pl.BlockSpec((1, tm, wa_n), lambda b, i: (b, i, 0)),
                  pl.BlockSpec((1, tm, wb_n), lambda b, i: (b, i, 0)),
                  pl.BlockSpec((wa_n, d), lambda b, i: (0, 0)),
                  pl.BlockSpec((wb_n, d), lambda b, i: (0, 0)),
                  pl.BlockSpec((1, tm, d), lambda b, i: (b, i, 0)),
                  vec(),
                  pl.BlockSpec((1, d), lambda b, i: (0, 0)),
                  vec(), vec()],
        out_specs=[pl.BlockSpec((1, tm, d), lambda b, i: (b, i, 0)),
                   pl.BlockSpec((d, tm), lambda b, i: (0, b * n_t + i)),
                   pl.BlockSpec((d, tm), lambda b, i: (0, b * n_t + i)),
                   pl.BlockSpec((2, tm), lambda b, i: (0, b * n_t + i))],
        out_shape=[jax.ShapeDtypeStruct((bsz, s_len, d), F32),
                   jax.ShapeDtypeStruct((d, bsz * s_len), BF16),
                   jax.ShapeDtypeStruct((d, bsz * s_len), FP8),
                   jax.ShapeDtypeStruct((2, bsz * s_len), F32)],
        compiler_params=_cparams(("arbitrary", "arbitrary")),
        name="mid0",
    )(a_out, b_out, w_a, w_b, x, g1, g, sc, sh)


def _odd_kernel(xp_ref, pt_ref, g2p_ref, g_ref, sc_ref, sh_ref, wu_ref, wv_ref, lng_ref, ws_ref, bs_ref, wo_ref,
                g1_ref, g2n_ref, sc2_ref, sh2_ref, x1_ref, ht_ref, h8_ref, rs_ref, h_scr, acc_scr):
    gi = pl.program_id(2)
    n_groups = pl.num_programs(2)

    @pl.when(gi == 0)
    def _():
        x = xp_ref[0] + g2p_ref[0] * pt_ref[...].T
        x1_ref[0] = x
        h_scr[...] = _ada_norm(x, g_ref[...], sc_ref[0], sh_ref[0]).astype(BF16)
        acc_scr[...] = jnp.zeros_like(acc_scr)

    h = h_scr[...]
    zu = _gelu(jnp.dot(h, wu_ref[...], preferred_element_type=F32))
    zv = _gelu(jnp.dot(h, wv_ref[...], preferred_element_type=F32))
    mu = jnp.mean(zv, axis=-1, keepdims=True)
    cen = zv - mu
    var = jnp.mean(cen * cen, axis=-1, keepdims=True)
    vn = (cen * lax.rsqrt(var + EPS) * lng_ref[0]).astype(BF16)
    ws = ws_ref[0]
    bias = bs_ref[0]
    tm = zu.shape[0]
    gated = []
    for c in range(tm // SGU_CHUNK):
        sl = slice(c * SGU_CHUNK, (c + 1) * SGU_CHUNK)
        s = jnp.dot(ws, vn[sl], preferred_element_type=F32) + bias
        gated.append((zu[sl] * s).astype(BF16))
    gated = jnp.concatenate(gated, axis=0)
    acc_scr[...] += jnp.dot(gated, wo_ref[...], preferred_element_type=F32)

    @pl.when(gi == n_groups - 1)
    def _():
        x1 = x1_ref[0] + g1_ref[0] * acc_scr[...]
        x1_ref[0] = x1
        h2 = _ada_norm(x1, g2n_ref[...], sc2_ref[0], sh2_ref[0])
        _store_peer_inputs(h2, ht_ref, h8_ref, rs_ref)


def _odd(xp, peer_t, g2p, g, sc, sh, w_in, ln_g, w_s, b_s, w_out, g1, g2n, sc2, sh2, tm=512):
    bsz, s_len, d = xp.shape
    n_t = s_len // tm
    width = w_out.shape[0]
    gw = width // SGU_GROUPS
    vec = lambda: pl.BlockSpec((1, 1, d), lambda b, i, gi: (b, 0, 0))
    row = lambda: pl.BlockSpec((1, d), lambda b, i, gi: (0, 0))
    bias = jnp.broadcast_to(b_s.astype(F32)[:, :, None], (SGU_GROUPS, SGU_CHUNK, gw))
    return pl.pallas_call(
        _odd_kernel,
        grid=(bsz, n_t, SGU_GROUPS),
        in_specs=[pl.BlockSpec((1, tm, d), lambda b, i, gi: (b, i, 0)),
                  pl.BlockSpec((d, tm), lambda b, i, gi: (0, b * n_t + i)),
                  vec(),
                  row(), vec(), vec(),
                  pl.BlockSpec((d, gw), lambda b, i, gi: (0, gi)),
                  pl.BlockSpec((d, gw), lambda b, i, gi: (0, SGU_GROUPS + gi)),
                  pl.BlockSpec((1, 1, gw), lambda b, i, gi: (gi, 0, 0)),
                  pl.BlockSpec((1, SGU_CHUNK, SGU_CHUNK), lambda b, i, gi: (gi, 0, 0)),
                  pl.BlockSpec((1, SGU_CHUNK, gw), lambda b, i, gi: (gi, 0, 0)),
                  pl.BlockSpec((gw, d), lambda b, i, gi: (gi, 0)),
                  vec(), row(), vec(), vec()],
        out_specs=[pl.BlockSpec((1, tm, d), lambda b, i, gi: (b, i, 0)),
                   pl.BlockSpec((d, tm), lambda b, i, gi: (0, b * n_t + i)),
                   pl.BlockSpec((d, tm), lambda b, i, gi: (0, b * n_t + i)),
                   pl.BlockSpec((2, tm), lambda b, i, gi: (0, b * n_t + i))],
        out_shape=[jax.ShapeDtypeStruct((bsz, s_len, d), F32),
                   jax.ShapeDtypeStruct((d, bsz * s_len), BF16),
                   jax.ShapeDtypeStruct((d, bsz * s_len), FP8),
                   jax.ShapeDtypeStruct((2, bsz * s_len), F32)],
        scratch_shapes=[pltpu.VMEM((tm, d), BF16), pltpu.VMEM((tm, d), F32)],
        compiler_params=_cparams(("arbitrary", "arbitrary", "arbitrary")),
        name="odd_mixer",
    )(xp, peer_t, g2p, g, sc, sh, w_in, w_in, ln_g.reshape(SGU_GROUPS, 1, gw), w_s, bias, w_out, g1, g2n, sc2, sh2)


def _compose_kernel(k_ref, w_ref, o_ref):
    o_ref[0, 0] = lax.dot_general(k_ref[0], w_ref[...], (((1,), (1,)), ((), ())),
                                  preferred_element_type=F32, precision=HP)


def _compose_score_weights(w_q, k1, k2):
    d = w_q.shape[0]
    dk = k1.shape[1]
    keys = jnp.stack([k1, k2]).astype(F32)
    out = pl.pallas_call(
        _compose_kernel,
        grid=(2, PEER_HEADS),
        in_specs=[pl.BlockSpec((1, PEER_NKEYS, dk), lambda a, h: (a, 0, 0)),
                  pl.BlockSpec((d, dk), lambda a, h: (0, h * 2 + a))],
        out_specs=pl.BlockSpec((1, 1, PEER_NKEYS, d), lambda a, h: (a, h, 0, 0)),
        out_shape=jax.ShapeDtypeStruct((2, PEER_HEADS, PEER_NKEYS, d), F32),
        compiler_params=_cparams(("arbitrary", "arbitrary")),
        name="compose_scores",
    )(keys, w_q.astype(F32))
    return out.transpose(0, 2, 1, 3).reshape(2 * PEER_NKEYS * PEER_HEADS, d).astype(BF16)


def _batcher_pairs(n):
    pairs = []
    p = 1
    while p < n:
        k = p
        while k >= 1:
            for j in range(k % p, n - k, 2 * k):
                for i in range(min(k, n - j - k)):
                    if (i + j) // (2 * p) == (i + j + k) // (2 * p):
                        pairs.append((i + j, i + j + k))
            k //= 2
        p *= 2
    return pairs


_SORT16 = _batcher_pairs(PEER_TOPK)


def _vmax(a, b):
    if a is None:
        return b
    if b is None:
        return a
    return jnp.maximum(a, b)


def _vmin(a, b):
    if a is None or b is None:
        return None
    return jnp.minimum(a, b)


def _sort16_desc(v):
    v = list(v)
    for i, j in _SORT16:
        v[i], v[j] = _vmax(v[i], v[j]), _vmin(v[i], v[j])
    return v


def _merge_top16(a, b):
    n = PEER_TOPK
    c = [_vmax(a[i], b[n - 1 - i]) for i in range(n)]
    d = n // 2
    while d >= 1:
        for i in range(n):
            if (i & d) == 0:
                c[i], c[i + d] = _vmax(c[i], c[i + d]), _vmin(c[i], c[i + d])
        d //= 2
    return c


def _top16_sorted(vals):
    groups = [_sort16_desc(vals[i:i + PEER_TOPK]) for i in range(0, len(vals), PEER_TOPK)]
    while len(groups) > 1:
        groups = [_merge_top16(groups[i], groups[i + 1]) for i in range(0, len(groups), 2)]
    return groups[0]


def _route_kernel(ht_ref, w_ref, cnt_ref, p1_ref, r2_ref, p2_ref, r2_scr, p2_scr, *, tw):
    for sp in range(ht_ref.shape[1] // tw):
        _route_block(ht_ref, w_ref, cnt_ref, p1_ref, r2_ref, p2_ref, r2_scr, p2_scr, sp * tw, tw)


def _route_block(ht_ref, w_ref, cnt_ref, p1_ref, r2_ref, p2_ref, r2_scr, p2_scr, lane0, tw):
    nh, nk = PEER_HEADS, PEER_NKEYS
    tok = slice(lane0, lane0 + tw)
    lb0 = lane0 // LANES
    st = jnp.dot(w_ref[...], ht_ref[:, tok], preferred_element_type=F32)
    s1 = [st[i * nh:(i + 1) * nh] for i in range(nk)]
    s2 = [st[(nk + i) * nh:(nk + i + 1) * nh] for i in range(nk)]
    a = _top16_sorted(s1)
    b = _top16_sorted(s2)
    cands = [[a[k] + b[l] for l in range(PEER_TOPK // (k + 1))] for k in range(PEER_TOPK)]
    flat = [c for rowc in cands[1:] for c in rowc]
    flat = flat + [None] * (-len(flat) % PEER_TOPK)
    groups = [cands[0]] + [_sort16_desc(flat[i:i + PEER_TOPK]) for i in range(0, len(flat), PEER_TOPK)]
    top = groups[0]
    for gr in groups[1:]:
        top = _merge_top16(top, gr)
    theta = top[PEER_TOPK - 1]
    z = jnp.ones_like(theta)
    for t in top[1:]:
        z = z + jnp.exp(t - top[0])
    inv_z = 1.0 / z
    n_lb = tw // LANES
    cnt_k = []
    for k in range(PEER_TOPK):
        c = jnp.zeros_like(theta)
        for cand in cands[k]:
            c = c + jnp.where(cand >= theta, 1.0, 0.0)
        cnt_k.append(c)
    for i in range(nk):
        rows = slice(i * nh, (i + 1) * nh)
        cnt = jnp.zeros_like(theta)
        rank = jnp.full_like(theta, float(PEER_TOPK))
        for l in range(PEER_TOPK - 1, -1, -1):
            cnt = jnp.where(s1[i] >= a[l], cnt_k[l], cnt)
            rank = jnp.where(s2[i] >= b[l], float(l), rank)
        cnt_ref[rows, tok] = cnt
        p1_ref[rows, tok] = jnp.exp(s1[i] - a[0]) * inv_z
        p2 = jnp.exp(s2[i] - b[0])
        for lb in range(n_lb):
            lanes = slice(lb * LANES, (lb + 1) * LANES)
            r2_scr[lb0 + lb, rows, :] = rank[:, lanes]
            p2_scr[lb0 + lb, rows, :] = p2[:, lanes]
    for h in range(nh):
        for lb in range(lb0, lb0 + n_lb):
            lanes = slice(lb * LANES, (lb + 1) * LANES)
            r2_ref[h, :, lanes] = r2_scr.at[lb][pl.ds(h, nk, stride=nh), :].astype(BF16)
            p2_ref[h, :, lanes] = p2_scr.at[lb][pl.ds(h, nk, stride=nh), :].astype(BF16)


def _route(ht, w_s, tm=512, tw=256):
    d, t = ht.shape
    nh, nk = PEER_HEADS, PEER_NKEYS
    rows = nh * nk
    tw = min(tw, tm)
    return pl.pallas_call(
        functools.partial(_route_kernel, tw=tw),
        grid=(t // tm,),
        in_specs=[pl.BlockSpec((d, tm), lambda i: (0, i)),
                  pl.BlockSpec((2 * rows, d), lambda i: (0, 0))],
        out_specs=[pl.BlockSpec((rows, tm), lambda i: (0, i)),
                   pl.BlockSpec((rows, tm), lambda i: (0, i)),
                   pl.BlockSpec((nh, nk, tm), lambda i: (0, 0, i)),
                   pl.BlockSpec((nh, nk, tm), lambda i: (0, 0, i))],
        out_shape=[jax.ShapeDtypeStruct((rows, t), F32),
                   jax.ShapeDtypeStruct((rows, t), F32),
                   jax.ShapeDtypeStruct((nh, nk, t), BF16),
                   jax.ShapeDtypeStruct((nh, nk, t), BF16)],
        scratch_shapes=[pltpu.VMEM((tm // LANES, rows, LANES), F32), pltpu.VMEM((tm // LANES, rows, LANES), F32)],
        compiler_params=_cparams(("arbitrary",)),
        name="peer_route",
    )(ht, w_s)


def _peer_kernel(h8_ref, rs_ref, u_ref, vt_ref, cnt_ref, p1_ref, r2_ref, p2_ref, o_ref, *, n_split):
    ei = pl.program_id(1)
    nh, nk = PEER_HEADS, PEER_NKEYS
    te = u_ref.shape[0]
    ts = te // n_split

    @pl.when(ei == 0)
    def _():
        o_ref[...] = jnp.zeros_like(o_ref)

    gates = []
    for ii in range(te // nk):
        key_i = ei * (te // nk) + ii
        gate = None
        for h in range(nh):
            r = key_i * nh + h
            cnt = cnt_ref[pl.ds(r, 1), :].astype(BF16)
            p1 = p1_ref[pl.ds(r, 1), :].astype(BF16)
            p2 = p2_ref[h]
            term = jnp.where(r2_ref[h] < cnt, p2, jnp.zeros_like(p2)) * p1
            gate = term if gate is None else gate + term
        gates.append(gate)

    wts = []
    for sp in range(n_split):
        act = jnp.dot(u_ref[sp * ts:(sp + 1) * ts, :], h8_ref[...], preferred_element_type=F32) * rs_ref[0:1, :]
        gate = jnp.concatenate(gates[sp * (ts // nk):(sp + 1) * (ts // nk)], axis=0)
        wts.append((_gelu(act.astype(BF16)) * gate * rs_ref[1:2, :].astype(BF16)).astype(FP8))
    wt = jnp.concatenate(wts, axis=0)
    o_ref[...] += jnp.dot(vt_ref[...], wt, preferred_element_type=F32)

    @pl.when(ei == pl.num_programs(1) - 1)
    def _():
        o_ref[...] = o_ref[...] * rs_ref[2:3, :]


def _peer(h8, rs, u_tabs, vt_tabs, layer, cnt, p1, r2, p2, tm=512, te=1024, n_split=8):
    d, t = h8.shape
    n_e = u_tabs.shape[1]
    nh, nk = PEER_HEADS, PEER_NKEYS
    rows = nh * nk
    return pl.pallas_call(
        functools.partial(_peer_kernel, n_split=n_split),
        grid=(t // tm, n_e // te),
        in_specs=[pl.BlockSpec((d, tm), lambda i, e: (0, i)),
                  pl.BlockSpec((3, tm), lambda i, e: (0, i)),
                  pl.BlockSpec((None, te, d), lambda i, e: (layer, e, 0)),
                  pl.BlockSpec((None, d, te), lambda i, e: (layer, 0, e)),
                  pl.BlockSpec((rows, tm), lambda i, e: (0, i)),
                  pl.BlockSpec((rows, tm), lambda i, e: (0, i)),
                  pl.BlockSpec((nh, nk, tm), lambda i, e: (0, 0, i)),
                  pl.BlockSpec((nh, nk, tm), lambda i, e: (0, 0, i))],
        out_specs=pl.BlockSpec((d, tm), lambda i, e: (0, i)),
        out_shape=jax.ShapeDtypeStruct((d, t), F32),
        compiler_params=_cparams(("arbitrary", "arbitrary")),
        name="peer_dense",
    )(h8, rs, u_tabs, vt_tabs, cnt, p1, r2, p2)


def _resid_kernel(x_ref, pt_ref, g2_ref, fg_ref, o_ref, *, final_norm):
    x2 = x_ref[0] + g2_ref[0] * pt_ref[...].T
    if final_norm:
        ms = jnp.mean(x2 * x2, axis=-1, keepdims=True)
        x2 = x2 * lax.rsqrt(ms + EPS) * fg_ref[...]
    o_ref[0] = x2


def _resid(x1, peer_t, g2, final_g, final_norm, tm=256):
    bsz, s_len, d = x1.shape
    n_t = s_len // tm
    return pl.pallas_call(
        functools.partial(_resid_kernel, final_norm=final_norm),
        grid=(bsz, n_t),
        in_specs=[pl.BlockSpec((1, tm, d), lambda b, i: (b, i, 0)),
                  pl.BlockSpec((d, tm), lambda b, i: (0, b * n_t + i)),
                  pl.BlockSpec((1, 1, d), lambda b, i: (b, 0, 0)),
                  pl.BlockSpec((1, d), lambda b, i: (0, 0))],
        out_specs=pl.BlockSpec((1, tm, d), lambda b, i: (b, i, 0)),
        out_shape=jax.ShapeDtypeStruct((bsz, s_len, d), F32),
        compiler_params=_cparams(("arbitrary", "arbitrary")),
        name="peer_residual",
    )(x1, peer_t, g2, final_g)


def _peer_layer(ht, h8, rs, w_q, k1, k2, u_tabs, u_unscale, u_rownorm, vt_tabs, v_unscale, layer):
    w_s = _compose_score_weights(w_q, k1, k2)
    cnt, p1, r2, p2 = _route(ht, w_s)
    bound = jnp.maximum(PEER_HEADS * u_rownorm[layer] * rs[1:2], FP8_TINY)
    w_scale = jnp.exp2(jnp.floor(jnp.log2(FP8_TARGET / bound)))
    scales = jnp.concatenate([rs[0:1] * u_unscale[layer], w_scale, v_unscale[layer] / w_scale], axis=0)
    return _peer(h8, scales, u_tabs, vt_tabs, layer, cnt, p1, r2, p2)


def _rope_tables(s_len):
    rows = s_len // GRID_W
    row = jnp.repeat(jnp.arange(rows, dtype=F32), GRID_W)
    col = jnp.tile(jnp.arange(GRID_W, dtype=F32), rows)
    n_pair_axis = HEAD_DIM // 4
    inv_freq = ROPE_THETA ** (-jnp.arange(n_pair_axis, dtype=F32) / n_pair_axis)
    ang = jnp.concatenate([row[:, None] * inv_freq, col[:, None] * inv_freq], axis=-1)
    cos, sin = jnp.cos(ang), jnp.sin(ang)
    return jnp.concatenate([cos, cos], axis=-1), jnp.concatenate([-sin, sin], axis=-1)


def _deinterleave_perm():
    half = HEAD_DIM // 2
    return np.concatenate([2 * np.arange(half), 2 * np.arange(half) + 1])


def kernel(x, c, ada_w, ada_b, norm_g, even_w_in, even_w_out, s5_lam_re, s5_lam_im, s5_log_dt, s5_b_re, s5_b_im, s5_c_re, s5_c_im, s5_d, s5_glu_w, s5_glu_b, q_norm_g, k_norm_g, odd_w_in, odd_w_out, sgu_ln_g, sgu_w, sgu_b, peer_w_q, peer_k1, peer_k2, peer_u, peer_v, final_g):
    bsz, s_len, d = x.shape
    depth = ada_w.shape[0]
    mod = _modulation(c, ada_w, ada_b)
    fg = final_g.reshape(1, d)
    n_a = s5_lam_re.shape[2] * S5_GROUP
    n_q = N_Q_HEADS * HEAD_DIM
    n_kv = N_KV_HEADS * HEAD_DIM
    n_steps = max(1, int(math.log2(s_len // S5_CHUNK)))
    cos_t, sin_t = _rope_tables(s_len)
    perm = _deinterleave_perm()
    u_amax = jnp.maximum(jnp.max(jnp.abs(peer_u), axis=(1, 2)), FP8_TINY)
    u_tabs = (peer_u * (FP8_TARGET / u_amax)[:, None, None]).astype(FP8)
    u_unscale = u_amax * (1.0 / FP8_TARGET)
    u_rownorm = jnp.sqrt(jnp.max(jnp.sum(peer_u * peer_u, axis=2), axis=1))
    v_amax = jnp.maximum(jnp.max(jnp.abs(peer_v), axis=(1, 2)), FP8_TINY)
    vt_tabs = (jnp.swapaxes(peer_v, 1, 2) * (FP8_TARGET / v_amax)[:, None, None]).astype(FP8)
    v_unscale = v_amax * (1.0 / FP8_TARGET)

    pending = None
    for layer in range(depth):
        sh1, sc1, g1, sh2, sc2, g2 = [m.reshape(bsz, 1, d) for m in jnp.split(mod[layer], 6, axis=-1)]
        ng1 = norm_g[layer, 0].reshape(1, d)
        ng2 = norm_g[layer, 1].reshape(1, d)
        i = layer // 2
        if layer % 2 == 0:
            if pending is not None:
                x = _resid(*pending, fg, False)
            w_in = even_w_in[i]
            col = np.arange(w_in.shape[1])
            for hd in range(N_Q_HEADS + N_KV_HEADS):
                lo = n_a + hd * HEAD_DIM
                col[lo:lo + HEAD_DIM] = lo + perm
            w_in_p = w_in[:, col].astype(BF16)
            gq = (q_norm_g[i][perm] * (HEAD_DIM ** -0.5)).reshape(1, HEAD_DIM)
            gk = k_norm_g[i][perm].reshape(1, HEAD_DIM)
            ua, q, k, v = _front0(x, ng1, sc1, sh1, w_in_p, gq, gk, cos_t, sin_t, n_a, n_q, n_kv)
            weights = _s5_weights(s5_lam_re[i], s5_lam_im[i], s5_log_dt[i], s5_b_re[i], s5_b_im[i],
                                  s5_c_re[i], s5_c_im[i], s5_d[i], s5_glu_w[i], s5_glu_b[i], n_steps)
            a_out = _s5(ua, weights, n_steps)
            b_out = _attention(q, k, v)
            w_out = even_w_out[i].astype(BF16)
            x1, ht, h8, rs = _mid0(a_out, b_out, w_out[:n_a], w_out[n_a:], x, g1, ng2, sc2, sh2)
        else:
            x1, ht, h8, rs = _odd(*pending, ng1, sc1, sh1, odd_w_in[i].astype(BF16), sgu_ln_g[i],
                                  sgu_w[i].astype(BF16), sgu_b[i], odd_w_out[i].astype(BF16), g1, ng2, sc2, sh2)
        peer_t = _peer_layer(ht, h8, rs, peer_w_q[layer], peer_k1[layer], peer_k2[layer],
                             u_tabs, u_unscale, u_rownorm, vt_tabs, v_unscale, layer)
        pending = (x1, peer_t, g2)
    return _resid(*pending, fg, True)
```

```python
import functools
import math

import numpy as np
import jax
import jax.numpy as jnp
from jax import lax
from jax.experimental import pallas as pl
from jax.experimental.pallas import tpu as pltpu

F32 = jnp.float32
BF16 = jnp.bfloat16
FP8 = jnp.float8_e4m3fn
FP8_TARGET = 224.0
FP8_TINY = 1e-30
EPS = 1e-6
HP = lax.Precision.HIGHEST

V7X_VMEM_LIMIT_BYTES = 56 * 1024 * 1024
LANES = 128
SUBLANES = 8

GRID_W = 64
S5_GROUP = 16
S5_STATE = 64
S5_CHUNK = 16
HEAD_DIM = 128
N_Q_HEADS = 8
N_KV_HEADS = 2
GQA_GROUP = N_Q_HEADS // N_KV_HEADS
ROPE_THETA = 10000.0
SGU_CHUNK = 128
SGU_GROUPS = 8
PEER_HEADS = 8
PEER_NKEYS = 128
PEER_TOPK = 16
DT_CLAMP = -1e-4


def _cparams(semantics):
    return pltpu.CompilerParams(dimension_semantics=semantics, vmem_limit_bytes=V7X_VMEM_LIMIT_BYTES)


def _gelu(x):
    return jax.nn.gelu(x)


def _mod_kernel(c_ref, w_ref, b_ref, o_ref):
    c = c_ref[...]
    ca = c * jax.nn.sigmoid(c)
    o_ref[0] = jnp.dot(ca, w_ref[0], preferred_element_type=F32, precision=HP) + b_ref[0]


def _modulation(c, ada_w, ada_b):
    depth, d, n = ada_w.shape
    bsz = c.shape[0]
    cpad = jnp.zeros((SUBLANES, d), F32).at[:bsz].set(c)
    tn = 1024
    out = pl.pallas_call(
        _mod_kernel,
        grid=(depth, n // tn),
        in_specs=[pl.BlockSpec((SUBLANES, d), lambda l, j: (0, 0)),
                  pl.BlockSpec((1, d, tn), lambda l, j: (l, 0, j)),
                  pl.BlockSpec((1, 1, tn), lambda l, j: (l, 0, j))],
        out_specs=pl.BlockSpec((1, SUBLANES, tn), lambda l, j: (l, 0, j)),
        out_shape=jax.ShapeDtypeStruct((depth, SUBLANES, n), F32),
        compiler_params=_cparams(("arbitrary", "arbitrary")),
        name="modulation",
    )(cpad, ada_w, ada_b.reshape(depth, 1, n))
    return out[:, :bsz]


def _ada_norm(x, g, sc, sh):
    ms = jnp.mean(x * x, axis=-1, keepdims=True)
    return x * lax.rsqrt(ms + EPS) * (g * (1.0 + sc)) + sh


def _front0_kernel(x_ref, g_ref, sc_ref, sh_ref, w_ref, gq_ref, gk_ref, cos_ref, sin_ref,
                   ua_ref, q_ref, k_ref, v_ref, *, n_a, n_q, n_kv):
    h = _ada_norm(x_ref[0], g_ref[...], sc_ref[0], sh_ref[0])
    z = jnp.dot(h.astype(BF16), w_ref[...], preferred_element_type=F32)
    ua_ref[0] = z[:, :n_a]
    cos = cos_ref[...]
    sin = sin_ref[...]

    def norm_rope(y, g):
        y = y * lax.rsqrt(jnp.mean(y * y, axis=-1, keepdims=True) + EPS) * g
        return y * cos + pltpu.roll(y, HEAD_DIM // 2, 1) * sin

    for hd in range(n_q // HEAD_DIM):
        lo = n_a + hd * HEAD_DIM
        q_ref[0, :, hd * HEAD_DIM:(hd + 1) * HEAD_DIM] = norm_rope(z[:, lo:lo + HEAD_DIM], gq_ref[...]).astype(BF16)
    for hd in range(n_kv // HEAD_DIM):
        lo = n_a + n_q + hd * HEAD_DIM
        k_ref[0, :, hd * HEAD_DIM:(hd + 1) * HEAD_DIM] = norm_rope(z[:, lo:lo + HEAD_DIM], gk_ref[...]).astype(BF16)
    v_ref[0] = z[:, n_a + n_q + n_kv:].astype(BF16)


def _front0(x, g, sc, sh, w_in, gq, gk, cos_t, sin_t, n_a, n_q, n_kv, tm=256):
    bsz, s_len, d = x.shape
    n_all = w_in.shape[1]
    vec = lambda: pl.BlockSpec((1, 1, d), lambda b, i: (b, 0, 0))
    return pl.pallas_call(
        functools.partial(_front0_kernel, n_a=n_a, n_q=n_q, n_kv=n_kv),
        grid=(bsz, s_len // tm),
        in_specs=[pl.BlockSpec((1, tm, d), lambda b, i: (b, i, 0)),
                  pl.BlockSpec((1, d), lambda b, i: (0, 0)),
                  vec(), vec(),
                  pl.BlockSpec((d, n_all), lambda b, i: (0, 0)),
                  pl.BlockSpec((1, HEAD_DIM), lambda b, i: (0, 0)),
                  pl.BlockSpec((1, HEAD_DIM), lambda b, i: (0, 0)),
                  pl.BlockSpec((tm, HEAD_DIM), lambda b, i: (i, 0)),
                  pl.BlockSpec((tm, HEAD_DIM), lambda b, i: (i, 0))],
        out_specs=[pl.BlockSpec((1, tm, n_a), lambda b, i: (b, i, 0)),
                   pl.BlockSpec((1, tm, n_q), lambda b, i: (b, i, 0)),
                   pl.BlockSpec((1, tm, n_kv), lambda b, i: (b, i, 0)),
                   pl.BlockSpec((1, tm, n_kv), lambda b, i: (b, i, 0))],
        out_shape=[jax.ShapeDtypeStruct((bsz, s_len, n_a), F32),
                   jax.ShapeDtypeStruct((bsz, s_len, n_q), BF16),
                   jax.ShapeDtypeStruct((bsz, s_len, n_kv), BF16),
                   jax.ShapeDtypeStruct((bsz, s_len, n_kv), BF16)],
        compiler_params=_cparams(("arbitrary", "arbitrary")),
        name="front0",
    )(x, g, sc, sh, w_in, gq, gk, cos_t, sin_t)


def _taps_kernel(cr_ref, ci_ref, br_ref, bi_ref, o_ref):
    for j in range(cr_ref.shape[0]):
        o_ref[j] = (jnp.dot(cr_ref[j], br_ref[j], preferred_element_type=F32, precision=HP)
                    - jnp.dot(ci_ref[j], bi_ref[j], preferred_element_type=F32, precision=HP))


def _s5_taps(cp_r, cp_i, bb_r, bb_i, per_step=8):
    n_b, rows, n_p = cp_r.shape
    n_h = bb_r.shape[-1]
    per_step = min(per_step, n_b)
    lhs = pl.BlockSpec((per_step, rows, n_p), lambda i: (i, 0, 0))
    rhs = pl.BlockSpec((per_step, n_p, n_h), lambda i: (i, 0, 0))
    return pl.pallas_call(
        _taps_kernel,
        grid=(n_b // per_step,),
        in_specs=[lhs, lhs, rhs, rhs],
        out_specs=pl.BlockSpec((per_step, rows, n_h), lambda i: (i, 0, 0)),
        out_shape=jax.ShapeDtypeStruct((n_b, rows, n_h), F32),
        compiler_params=_cparams(("arbitrary",)),
        name="s5_taps",
    )(cp_r, cp_i, bb_r, bb_i)


def _s5_weights(lam_re, lam_im, log_dt, b_re, b_im, c_re, c_im, d_skip, glu_w, glu_b, n_steps):
    L = S5_CHUNK
    n_g, n_p = lam_re.shape[1], lam_re.shape[2]
    n_h = b_re.shape[-1]
    lr = jnp.minimum(lam_re.astype(F32), DT_CLAMP)
    li = lam_im.astype(F32)
    dt = jnp.exp(log_dt.astype(F32))[..., None]
    mag = jnp.exp(lr * dt)
    ar = mag * jnp.cos(li * dt)
    ai = mag * jnp.sin(li * dt)
    den = lr * lr + li * li
    nr = ar - 1.0
    cr = (nr * lr + ai * li) / den
    ci = (ai * lr - nr * li) / den
    bbr = cr[..., None] * b_re - ci[..., None] * b_im
    bbi = cr[..., None] * b_im + ci[..., None] * b_re
    pr, pi = [jnp.ones_like(ar)], [jnp.zeros_like(ai)]
    for _ in range(L):
        pr_n = pr[-1] * ar - pi[-1] * ai
        pi_n = pr[-1] * ai + pi[-1] * ar
        pr.append(pr_n)
        pi.append(pi_n)
    pw_r = jnp.stack(pr)
    pw_i = jnp.stack(pi)
    cre = c_re.astype(F32)[None]
    cim = c_im.astype(F32)[None]
    cp_r = cre * pw_r[:, :, :, None, :] - cim * pw_i[:, :, :, None, :]
    cp_i = cre * pw_i[:, :, :, None, :] + cim * pw_r[:, :, :, None, :]
    to_rows = lambda c: c[:L].transpose(1, 2, 0, 3, 4).reshape(2 * n_g, L * n_h, n_p)
    taps = _s5_taps(to_rows(cp_r), to_rows(cp_i), bbr.reshape(2 * n_g, n_p, n_h), bbi.reshape(2 * n_g, n_p, n_h))
    kern = taps.reshape(2, n_g, L, n_h, n_h).transpose(2, 0, 1, 3, 4)
    kf, kb = kern[:, 0], kern[:, 1]
    centre = kf[0] + kb[0] + d_skip.astype(F32)[:, :, None] * jnp.eye(n_h, dtype=F32)
    ktab = jnp.concatenate([kb[1:][::-1], centre[None], kf[1:]], axis=0)
    idx = np.arange(L)[:, None] - np.arange(L)[None, :] + L - 1
    m_full = ktab[idx]
    mt = m_full.transpose(2, 1, 4, 0, 3).reshape(n_g, L * n_h, L * n_h)

    def win_part(d, order):
        wr = pw_r[order, d][:, :, :, None] * bbr[d][None] - pw_i[order, d][:, :, :, None] * bbi[d][None]
        wi = pw_r[order, d][:, :, :, None] * bbi[d][None] + pw_i[order, d][:, :, :, None] * bbr[d][None]
        tr = lambda w: w.transpose(1, 0, 3, 2).reshape(n_g, L * n_h, n_p)
        return tr(wr), tr(wi)

    wf_r, wf_i = win_part(0, np.arange(L - 1, -1, -1))
    wb_r, wb_i = win_part(1, np.arange(L))
    win = jnp.concatenate([wf_r, wf_i, wb_r, wb_i], axis=-1)

    def wout_part(d, order):
        er = cp_r[order, d]
        ei = cp_i[order, d]
        tr = lambda w: w.transpose(1, 3, 0, 2).reshape(n_g, n_p, L * n_h)
        return tr(er), tr(-ei)

    of_r, of_i = wout_part(0, np.arange(1, L + 1))
    ob_r, ob_i = wout_part(1, np.arange(L, 0, -1))
    wout = jnp.concatenate([of_r, of_i, ob_r, ob_i], axis=1)

    glu = jnp.einsum('ts,ghk->gthsk', jnp.eye(L, dtype=F32), glu_w.astype(F32)).reshape(n_g, L * n_h, L * n_h)
    glub = jnp.tile(glu_b.astype(F32), (1, L)).reshape(n_g, 1, L * n_h)

    qr, qi = pw_r[L], pw_i[L]
    rows = []
    for _ in range(n_steps):
        rows.append((jnp.concatenate([qr, qr], -1), jnp.concatenate([-qi, qi], -1)))
        qr, qi = qr * qr - qi * qi, 2.0 * qr * qi
    coef = jnp.stack([jnp.stack([a1, a2], axis=0) for a1, a2 in rows], axis=0)
    coef = coef.transpose(3, 2, 0, 1, 4).reshape(n_g, 2 * n_steps * 2, 2 * n_p)
    return mt.astype(BF16), win.astype(BF16), wout.astype(BF16), glu.astype(BF16), glub, coef


def _s5_kernel(u_ref, perm_ref, mt_ref, win_ref, wout_ref, glu_ref, glub_ref, coef_ref, o_ref, *, n_steps):
    L = S5_CHUNK
    n_c = u_ref.shape[1] // L
    gpb, lh = mt_ref.shape[0], mt_ref.shape[1]
    ww = jnp.concatenate([u_ref.at[0][pl.ds(t, n_c, stride=L), :].astype(BF16) for t in range(L)], axis=1)
    xp = jnp.dot(ww, perm_ref[...], preferred_element_type=F32).astype(BF16)
    row = lax.broadcasted_iota(jnp.int32, (n_c, 2 * S5_STATE), 0)

    def scan(x, g, d):
        for k in range(n_steps):
            sh = 1 << k
            if d == 0:
                shifted = jnp.where(row >= sh, pltpu.roll(x, sh, 0), 0.0)
            else:
                shifted = jnp.where(row < n_c - sh, pltpu.roll(x, n_c - sh, 0), 0.0)
            base = (d * n_steps + k) * 2
            a1 = coef_ref[g, base:base + 1, :]
            a2 = coef_ref[g, base + 1:base + 2, :]
            x = x + a1 * shifted + a2 * pltpu.roll(shifted, S5_STATE, 1)
        if d == 0:
            return jnp.where(row >= 1, pltpu.roll(x, 1, 0), 0.0)
        return jnp.where(row < n_c - 1, pltpu.roll(x, n_c - 1, 0), 0.0)

    outs = []
    for g in range(gpb):
        u = xp[:, g * lh:(g + 1) * lh]
        st = jnp.dot(u, win_ref[g], preferred_element_type=F32)
        two_p = st.shape[1] // 2
        hcat = jnp.concatenate([scan(st[:, :two_p], g, 0), scan(st[:, two_p:], g, 1)], axis=1).astype(BF16)
        y = (jnp.dot(u, mt_ref[g], preferred_element_type=F32)
             + jnp.dot(hcat, wout_ref[g], preferred_element_type=F32))
        y = _gelu(y)
        gate = jax.nn.sigmoid(jnp.dot(y.astype(BF16), glu_ref[g], preferred_element_type=F32) + glub_ref[g])
        outs.append((y * gate).astype(BF16))
    back = lax.dot_general(jnp.concatenate(outs, axis=1), perm_ref[...], (((1,), (1,)), ((), ())),
                           preferred_element_type=F32)
    for t in range(L):
        o_ref.at[0][pl.ds(t, n_c, stride=L), :] = back[:, t * LANES:(t + 1) * LANES]


def _s5(ua, weights, n_steps):
    mt, win, wout, glu, glub, coef = weights
    bsz, s_len, width = ua.shape
    n_g = mt.shape[0]
    n_h = width // n_g
    L = S5_CHUNK
    gpb = LANES // n_h
    n_lb = width // LANES
    cols = L * LANES
    ci = np.arange(cols)
    t_i, g_i, h_i = ci // LANES, (ci % LANES) // n_h, ci % n_h
    perm = (jnp.asarray(g_i * (L * n_h) + t_i * n_h + h_i)[:, None] == jnp.arange(cols)[None, :]).astype(BF16)
    blk = lambda a: pl.BlockSpec((gpb,) + a.shape[1:], lambda b, lb: (lb, 0, 0))
    return pl.pallas_call(
        functools.partial(_s5_kernel, n_steps=n_steps),
        grid=(bsz, n_lb),
        in_specs=[pl.BlockSpec((1, s_len, LANES), lambda b, lb: (b, 0, lb)),
                  pl.BlockSpec((cols, cols), lambda b, lb: (0, 0)),
                  blk(mt), blk(win), blk(wout), blk(glu), blk(glub), blk(coef)],
        out_specs=pl.BlockSpec((1, s_len, LANES), lambda b, lb: (b, 0, lb)),
        out_shape=jax.ShapeDtypeStruct((bsz, s_len, width), F32),
        compiler_params=_cparams(("arbitrary", "arbitrary")),
        name="s5",
    )(ua, perm, mt, win, wout, glu, glub, coef)


def _attn_kernel(q_ref, k_ref, v_ref, o_ref, *, tk):
    tq = q_ref.shape[1]
    qs = jnp.concatenate([q_ref[0, :, h * HEAD_DIM:(h + 1) * HEAD_DIM] for h in range(GQA_GROUP)], axis=0)
    n_rows = qs.shape[0]
    n_kv = k_ref.shape[1] // tk

    def body(j, carry):
        m, l, acc = carry
        start = pl.multiple_of(j * tk, tk)
        kc = k_ref[0, pl.ds(start, tk), :]
        vc = v_ref[0, pl.ds(start, tk), :]
        s = lax.dot_general(qs, kc, (((1,), (1,)), ((), ())), preferred_element_type=F32)
        m_new = jnp.maximum(m, jnp.max(s, axis=-1, keepdims=True))
        p = jnp.exp(s - m_new)
        alpha = jnp.exp(m - m_new)
        l = alpha * l + jnp.sum(p, axis=-1, keepdims=True)
        acc = alpha * acc + jnp.dot(p.astype(BF16), vc, preferred_element_type=F32)
        return m_new, l, acc

    m0 = jnp.full((n_rows, 1), -jnp.inf, F32)
    l0 = jnp.zeros((n_rows, 1), F32)
    a0 = jnp.zeros((n_rows, HEAD_DIM), F32)
    _, l, acc = lax.fori_loop(0, n_kv, body, (m0, l0, a0), unroll=True)
    out = acc / l
    for h in range(GQA_GROUP):
        o_ref[0, :, h * HEAD_DIM:(h + 1) * HEAD_DIM] = out[h * tq:(h + 1) * tq].astype(BF16)


def _attention(q, k, v, tq=128, tk=2048):
    bsz, s_len, n_q = q.shape
    gw = GQA_GROUP * HEAD_DIM
    tk = min(tk, s_len)
    return pl.pallas_call(
        functools.partial(_attn_kernel, tk=tk),
        grid=(bsz, N_KV_HEADS, s_len // tq),
        in_specs=[pl.BlockSpec((1, tq, gw), lambda b, h, i: (b, i, h)),
                  pl.BlockSpec((1, s_len, HEAD_DIM), lambda b, h, i: (b, 0, h)),
                  pl.BlockSpec((1, s_len, HEAD_DIM), lambda b, h, i: (b, 0, h))],
        out_specs=pl.BlockSpec((1, tq, gw), lambda b, h, i: (b, i, h)),
        out_shape=jax.ShapeDtypeStruct((bsz, s_len, n_q), BF16),
        compiler_params=_cparams(("arbitrary", "arbitrary", "arbitrary")),
        name="attention",
    )(q, k, v)


def _store_peer_inputs(h2, ht_ref, h8_ref, rs_ref):
    h2t = h2.T
    ht_ref[...] = h2t.astype(BF16)
    amax = jnp.maximum(jnp.max(jnp.abs(h2t), axis=0, keepdims=True), FP8_TINY)
    h8_ref[...] = (h2t * (FP8_TARGET / amax)).astype(FP8)
    rs_ref[0:1, :] = amax * (1.0 / FP8_TARGET)
    rs_ref[1:2, :] = jnp.sqrt(jnp.sum(h2t * h2t, axis=0, keepdims=True))


def _mid0_kernel(a_ref, b_ref, wa_ref, wb_ref, x_ref, g1_ref, g_ref, sc_ref, sh_ref, x1_ref, ht_ref, h8_ref, rs_ref):
    m = (jnp.dot(a_ref[0].astype(BF16), wa_ref[...], preferred_element_type=F32)
         + jnp.dot(b_ref[0], wb_ref[...], preferred_element_type=F32))
    x1 = x_ref[0] + g1_ref[0] * m
    x1_ref[0] = x1
    h2 = _ada_norm(x1, g_ref[...], sc_ref[0], sh_ref[0])
    _store_peer_inputs(h2, ht_ref, h8_ref, rs_ref)


def _mid0(a_out, b_out, w_a, w_b, x, g1, g, sc, sh, tm=256):
    bsz, s_len, d = x.shape
    n_t = s_len // tm
    wa_n, wb_n = w_a.shape[0], w_b.shape[0]
    vec = lambda: pl.BlockSpec((1, 1, d), lambda b, i: (b, 0, 0))
    return pl.pallas_call(
        _mid0_kernel,
        grid=(bsz, n_t),
        in_specs=[pl.BlockSpec((1, tm, wa_n), lambda b, i: (b, i, 0)),
                  pl.BlockSpec((1, tm, wb_n), lambda b, i: (b, i, 0)),
                  pl.BlockSpec((wa_n, d), lambda b, i: (0, 0)),
                  pl.BlockSpec((wb_n, d), lambda b, i: (0, 0)),
                  pl.BlockSpec((1, tm, d), lambda b, i: (b, i, 0)),
                  vec(),
                  pl.BlockSpec((1, d), lambda b, i: (0, 0)),
                  vec(), vec()],
        out_specs=[pl.BlockSpec((1, tm, d), lambda b, i: (b, i, 0)),
                   pl.BlockSpec((d, tm), lambda b, i: (0, b * n_t + i)),
                   pl.BlockSpec((d, tm), lambda b, i: (0, b * n_t + i)),
                   pl.BlockSpec((2, tm), lambda b, i: (0, b * n_t + i))],
        out_shape=[jax.ShapeDtypeStruct((bsz, s_len, d), F32),
                   jax.ShapeDtypeStruct((d, bsz * s_len), BF16),
                   jax.ShapeDtypeStruct((d, bsz * s_len), FP8),
                   jax.ShapeDtypeStruct((2, bsz * s_len), F32)],
        compiler_params=_cparams(("arbitrary", "arbitrary")),
        name="mid0",
    )(a_out, b_out, w_a, w_b, x, g1, g, sc, sh)


def _odd_kernel(xp_ref, pt_ref, g2p_ref, g_ref, sc_ref, sh_ref, wu_ref, wv_ref, lng_ref, ws_ref, bs_ref, wo_ref,
                g1_ref, g2n_ref, sc2_ref, sh2_ref, x1_ref, ht_ref, h8_ref, rs_ref, h_scr, acc_scr):
    gi = pl.program_id(2)
    n_groups = pl.num_programs(2)

    @pl.when(gi == 0)
    def _():
        x = xp_ref[0] + g2p_ref[0] * pt_ref[...].T
        x1_ref[0] = x
        h_scr[...] = _ada_norm(x, g_ref[...], sc_ref[0], sh_ref[0]).astype(BF16)
        acc_scr[...] = jnp.zeros_like(acc_scr)

    h = h_scr[...]
    zu = _gelu(jnp.dot(h, wu_ref[...], preferred_element_type=F32))
    zv = _gelu(jnp.dot(h, wv_ref[...], preferred_element_type=F32))
    mu = jnp.mean(zv, axis=-1, keepdims=True)
    cen = zv - mu
    var = jnp.mean(cen * cen, axis=-1, keepdims=True)
    vn = (cen * lax.rsqrt(var + EPS) * lng_ref[0]).astype(BF16)
    ws = ws_ref[0]
    bias = bs_ref[0]
    tm = zu.shape[0]
    gated = []
    for c in range(tm // SGU_CHUNK):
        sl = slice(c * SGU_CHUNK, (c + 1) * SGU_CHUNK)
        s = jnp.dot(ws, vn[sl], preferred_element_type=F32) + bias
        gated.append((zu[sl] * s).astype(BF16))
    gated = jnp.concatenate(gated, axis=0)
    acc_scr[...] += jnp.dot(gated, wo_ref[...], preferred_element_type=F32)

    @pl.when(gi == n_groups - 1)
    def _():
        x1 = x1_ref[0] + g1_ref[0] * acc_scr[...]
        x1_ref[0] = x1
        h2 = _ada_norm(x1, g2n_ref[...], sc2_ref[0], sh2_ref[0])
        _store_peer_inputs(h2, ht_ref, h8_ref, rs_ref)


def _odd(xp, peer_t, g2p, g, sc, sh, w_in, ln_g, w_s, b_s, w_out, g1, g2n, sc2, sh2, tm=512):
    bsz, s_len, d = xp.shape
    n_t = s_len // tm
    width = w_out.shape[0]
    gw = width // SGU_GROUPS
    vec = lambda: pl.BlockSpec((1, 1, d), lambda b, i, gi: (b, 0, 0))
    row = lambda: pl.BlockSpec((1, d), lambda b, i, gi: (0, 0))
    bias = jnp.broadcast_to(b_s.astype(F32)[:, :, None], (SGU_GROUPS, SGU_CHUNK, gw))
    return pl.pallas_call(
        _odd_kernel,
        grid=(bsz, n_t, SGU_GROUPS),
        in_specs=[pl.BlockSpec((1, tm, d), lambda b, i, gi: (b, i, 0)),
                  pl.BlockSpec((d, tm), lambda b, i, gi: (0, b * n_t + i)),
                  vec(),
                  row(), vec(), vec(),
                  pl.BlockSpec((d, gw), lambda b, i, gi: (0, gi)),
                  pl.BlockSpec((d, gw), lambda b, i, gi: (0, SGU_GROUPS + gi)),
                  pl.BlockSpec((1, 1, gw), lambda b, i, gi: (gi, 0, 0)),
                  pl.BlockSpec((1, SGU_CHUNK, SGU_CHUNK), lambda b, i, gi: (gi, 0, 0)),
                  pl.BlockSpec((1, SGU_CHUNK, gw), lambda b, i, gi: (gi, 0, 0)),
                  pl.BlockSpec((gw, d), lambda b, i, gi: (gi, 0)),
                  vec(), row(), vec(), vec()],
        out_specs=[pl.BlockSpec((1, tm, d), lambda b, i, gi: (b, i, 0)),
                   pl.BlockSpec((d, tm), lambda b, i, gi: (0, b * n_t + i)),
                   pl.BlockSpec((d, tm), lambda b, i, gi: (0, b * n_t + i)),
                   pl.BlockSpec((2, tm), lambda b, i, gi: (0, b * n_t + i))],
        out_shape=[jax.ShapeDtypeStruct((bsz, s_len, d), F32),
                   jax.ShapeDtypeStruct((d, bsz * s_len), BF16),
                   jax.ShapeDtypeStruct((d, bsz * s_len), FP8),
                   jax.ShapeDtypeStruct((2, bsz * s_len), F32)],
        scratch_shapes=[pltpu.VMEM((tm, d), BF16), pltpu.VMEM((tm, d), F32)],
        compiler_params=_cparams(("arbitrary", "arbitrary", "arbitrary")),
        name="odd_mixer",
    )(xp, peer_t, g2p, g, sc, sh, w_in, w_in, ln_g.reshape(SGU_GROUPS, 1, gw), w_s, bias, w_out, g1, g2n, sc2, sh2)


def _compose_kernel(k_ref, w_ref, o_ref):
    o_ref[0, 0] = lax.dot_general(k_ref[0], w_ref[...], (((1,), (1,)), ((), ())),
                                  preferred_element_type=F32, precision=HP)


def _compose_score_weights(w_q, k1, k2):
    d = w_q.shape[0]
    dk = k1.shape[1]
    keys = jnp.stack([k1, k2]).astype(F32)
    out = pl.pallas_call(
        _compose_kernel,
        grid=(2, PEER_HEADS),
        in_specs=[pl.BlockSpec((1, PEER_NKEYS, dk), lambda a, h: (a, 0, 0)),
                  pl.BlockSpec((d, dk), lambda a, h: (0, h * 2 + a))],
        out_specs=pl.BlockSpec((1, 1, PEER_NKEYS, d), lambda a, h: (a, h, 0, 0)),
        out_shape=jax.ShapeDtypeStruct((2, PEER_HEADS, PEER_NKEYS, d), F32),
        compiler_params=_cparams(("arbitrary", "arbitrary")),
        name="compose_scores",
    )(keys, w_q.astype(F32))
    return out.transpose(0, 2, 1, 3).reshape(2 * PEER_NKEYS * PEER_HEADS, d).astype(BF16)


def _batcher_pairs(n):
    pairs = []
    p = 1
    while p < n:
        k = p
        while k >= 1:
            for j in range(k % p, n - k, 2 * k):
                for i in range(min(k, n - j - k)):
                    if (i + j) // (2 * p) == (i + j + k) // (2 * p):
                        pairs.append((i + j, i + j + k))
            k //= 2
        p *= 2
    return pairs


_SORT16 = _batcher_pairs(PEER_TOPK)


def _vmax(a, b):
    if a is None:
        return b
    if b is None:
        return a
    return jnp.maximum(a, b)


def _vmin(a, b):
    if a is None or b is None:
        return None
    return jnp.minimum(a, b)


def _sort16_desc(v):
    v = list(v)
    for i, j in _SORT16:
        v[i], v[j] = _vmax(v[i], v[j]), _vmin(v[i], v[j])
    return v


def _merge_top16(a, b):
    n = PEER_TOPK
    c = [_vmax(a[i], b[n - 1 - i]) for i in range(n)]
    d = n // 2
    while d >= 1:
        for i in range(n):
            if (i & d) == 0:
                c[i], c[i + d] = _vmax(c[i], c[i + d]), _vmin(c[i], c[i + d])
        d //= 2
    return c


def _top16_sorted(vals):
    groups = [_sort16_desc(vals[i:i + PEER_TOPK]) for i in range(0, len(vals), PEER_TOPK)]
    while len(groups) > 1:
        groups = [_merge_top16(groups[i], groups[i + 1]) for i in range(0, len(groups), 2)]
    return groups[0]


def _route_kernel(ht_ref, w_ref, cnt_ref, p1_ref, r2_ref, p2_ref, r2_scr, p2_scr, *, tw):
    for sp in range(ht_ref.shape[1] // tw):
        _route_block(ht_ref, w_ref, cnt_ref, p1_ref, r2_ref, p2_ref, r2_scr, p2_scr, sp * tw, tw)


def _route_block(ht_ref, w_ref, cnt_ref, p1_ref, r2_ref, p2_ref, r2_scr, p2_scr, lane0, tw):
    nh, nk = PEER_HEADS, PEER_NKEYS
    tok = slice(lane0, lane0 + tw)
    lb0 = lane0 // LANES
    st = jnp.dot(w_ref[...], ht_ref[:, tok], preferred_element_type=F32)
    s1 = [st[i * nh:(i + 1) * nh] for i in range(nk)]
    s2 = [st[(nk + i) * nh:(nk + i + 1) * nh] for i in range(nk)]
    a = _top16_sorted(s1)
    b = _top16_sorted(s2)
    cands = [[a[k] + b[l] for l in range(PEER_TOPK // (k + 1))] for k in range(PEER_TOPK)]
    flat = [c for rowc in cands[1:] for c in rowc]
    flat = flat + [None] * (-len(flat) % PEER_TOPK)
    groups = [cands[0]] + [_sort16_desc(flat[i:i + PEER_TOPK]) for i in range(0, len(flat), PEER_TOPK)]
    top = groups[0]
    for gr in groups[1:]:
        top = _merge_top16(top, gr)
    theta = top[PEER_TOPK - 1]
    z = jnp.ones_like(theta)
    for t in top[1:]:
        z = z + jnp.exp(t - top[0])
    inv_z = 1.0 / z
    n_lb = tw // LANES
    cnt_k = []
    for k in range(PEER_TOPK):
        c = jnp.zeros_like(theta)
        for cand in cands[k]:
            c = c + jnp.where(cand >= theta, 1.0, 0.0)
        cnt_k.append(c)
    for i in range(nk):
        rows = slice(i * nh, (i + 1) * nh)
        cnt = jnp.zeros_like(theta)
        rank = jnp.full_like(theta, float(PEER_TOPK))
        for l in range(PEER_TOPK - 1, -1, -1):
            cnt = jnp.where(s1[i] >= a[l], cnt_k[l], cnt)
            rank = jnp.where(s2[i] >= b[l], float(l), rank)
        cnt_ref[rows, tok] = cnt
        p1_ref[rows, tok] = jnp.exp(s1[i] - a[0]) * inv_z
        p2 = jnp.exp(s2[i] - b[0])
        for lb in range(n_lb):
            lanes = slice(lb * LANES, (lb + 1) * LANES)
            r2_scr[lb0 + lb, rows, :] = rank[:, lanes]
            p2_scr[lb0 + lb, rows, :] = p2[:, lanes]
    for h in range(nh):
        for lb in range(lb0, lb0 + n_lb):
            lanes = slice(lb * LANES, (lb + 1) * LANES)
            r2_ref[h, :, lanes] = r2_scr.at[lb][pl.ds(h, nk, stride=nh), :].astype(BF16)
            p2_ref[h, :, lanes] = p2_scr.at[lb][pl.ds(h, nk, stride=nh), :].astype(BF16)


def _route(ht, w_s, tm=512, tw=256):
    d, t = ht.shape
    nh, nk = PEER_HEADS, PEER_NKEYS
    rows = nh * nk
    tw = min(tw, tm)
    return pl.pallas_call(
        functools.partial(_route_kernel, tw=tw),
        grid=(t // tm,),
        in_specs=[pl.BlockSpec((d, tm), lambda i: (0, i)),
                  pl.BlockSpec((2 * rows, d), lambda i: (0, 0))],
        out_specs=[pl.BlockSpec((rows, tm), lambda i: (0, i)),
                   pl.BlockSpec((rows, tm), lambda i: (0, i)),
                   pl.BlockSpec((nh, nk, tm), lambda i: (0, 0, i)),
                   pl.BlockSpec((nh, nk, tm), lambda i: (0, 0, i))],
        out_shape=[jax.ShapeDtypeStruct((rows, t), F32),
                   jax.ShapeDtypeStruct((rows, t), F32),
                   jax.ShapeDtypeStruct((nh, nk, t), BF16),
                   jax.ShapeDtypeStruct((nh, nk, t), BF16)],
        scratch_shapes=[pltpu.VMEM((tm // LANES, rows, LANES), F32), pltpu.VMEM((tm // LANES, rows, LANES), F32)],
        compiler_params=_cparams(("arbitrary",)),
        name="peer_route",
    )(ht, w_s)


def _peer_kernel(h8_ref, rs_ref, u_ref, v_ref, cnt_ref, p1_ref, r2_ref, p2_ref, o_ref, *, n_split):
    ei = pl.program_id(1)
    nh, nk = PEER_HEADS, PEER_NKEYS
    te = u_ref.shape[0]
    ts = te // n_split

    @pl.when(ei == 0)
    def _():
        o_ref[...] = jnp.zeros_like(o_ref)

    gates = []
    for ii in range(te // nk):
        key_i = ei * (te // nk) + ii
        gate = None
        for h in range(nh):
            r = key_i * nh + h
            cnt = cnt_ref[pl.ds(r, 1), :].astype(BF16)
            p1 = p1_ref[pl.ds(r, 1), :].astype(BF16)
            p2 = p2_ref[h]
            term = jnp.where(r2_ref[h] < cnt, p2, jnp.zeros_like(p2)) * p1
            gate = term if gate is None else gate + term
        gates.append(gate)

    wts = []
    for sp in range(n_split):
        act = jnp.dot(u_ref[sp * ts:(sp + 1) * ts, :], h8_ref[...], preferred_element_type=F32) * rs_ref[0:1, :]
        gate = jnp.concatenate(gates[sp * (ts // nk):(sp + 1) * (ts // nk)], axis=0)
        wts.append((_gelu(act.astype(BF16)) * gate * rs_ref[1:2, :].astype(BF16)).astype(FP8))
    wt = jnp.concatenate(wts, axis=0)
    o_ref[...] += lax.dot_general(v_ref[...], wt, (((0,), (0,)), ((), ())), preferred_element_type=F32)

    @pl.when(ei == pl.num_programs(1) - 1)
    def _():
        o_ref[...] = o_ref[...] * rs_ref[2:3, :]


def _peer(h8, rs, u_tabs, v_tabs, layer, cnt, p1, r2, p2, tm=512, te=1024, n_split=8):
    d, t = h8.shape
    n_e = u_tabs.shape[1]
    nh, nk = PEER_HEADS, PEER_NKEYS
    rows = nh * nk
    return pl.pallas_call(
        functools.partial(_peer_kernel, n_split=n_split),
        grid=(t // tm, n_e // te),
        in_specs=[pl.BlockSpec((d, tm), lambda i, e: (0, i)),
                  pl.BlockSpec((3, tm), lambda i, e: (0, i)),
                  pl.BlockSpec((None, te, d), lambda i, e: (layer, e, 0)),
                  pl.BlockSpec((None, te, d), lambda i, e: (layer, e, 0)),
                  pl.BlockSpec((rows, tm), lambda i, e: (0, i)),
                  pl.BlockSpec((rows, tm), lambda i, e: (0, i)),
                  pl.BlockSpec((nh, nk, tm), lambda i, e: (0, 0, i)),
                  pl.BlockSpec((nh, nk, tm), lambda i, e: (0, 0, i))],
        out_specs=pl.BlockSpec((d, tm), lambda i, e: (0, i)),
        out_shape=jax.ShapeDtypeStruct((d, t), F32),
        compiler_params=_cparams(("arbitrary", "arbitrary")),
        name="peer_dense",
    )(h8, rs, u_tabs, v_tabs, cnt, p1, r2, p2)


def _resid_kernel(x_ref, pt_ref, g2_ref, fg_ref, o_ref, *, final_norm):
    x2 = x_ref[0] + g2_ref[0] * pt_ref[...].T
    if final_norm:
        ms = jnp.mean(x2 * x2, axis=-1, keepdims=True)
        x2 = x2 * lax.rsqrt(ms + EPS) * fg_ref[...]
    o_ref[0] = x2


def _resid(x1, peer_t, g2, final_g, final_norm, tm=256):
    bsz, s_len, d = x1.shape
    n_t = s_len // tm
    return pl.pallas_call(
        functools.partial(_resid_kernel, final_norm=final_norm),
        grid=(bsz, n_t),
        in_specs=[pl.BlockSpec((1, tm, d), lambda b, i: (b, i, 0)),
                  pl.BlockSpec((d, tm), lambda b, i: (0, b * n_t + i)),
                  pl.BlockSpec((1, 1, d), lambda b, i: (b, 0, 0)),
                  pl.BlockSpec((1, d), lambda b, i: (0, 0))],
        out_specs=pl.BlockSpec((1, tm, d), lambda b, i: (b, i, 0)),
        out_shape=jax.ShapeDtypeStruct((bsz, s_len, d), F32),
        compiler_params=_cparams(("arbitrary", "arbitrary")),
        name="peer_residual",
    )(x1, peer_t, g2, final_g)


def _peer_layer(ht, h8, rs, w_q, k1, k2, u_tabs, u_unscale, u_rownorm, v_tabs, v_unscale, layer):
    w_s = _compose_score_weights(w_q, k1, k2)
    cnt, p1, r2, p2 = _route(ht, w_s)
    bound = jnp.maximum(PEER_HEADS * u_rownorm[layer] * rs[1:2], FP8_TINY)
    w_scale = jnp.exp2(jnp.floor(jnp.log2(FP8_TARGET / bound)))
    scales = jnp.concatenate([rs[0:1] * u_unscale[layer], w_scale, v_unscale[layer] / w_scale], axis=0)
    return _peer(h8, scales, u_tabs, v_tabs, layer, cnt, p1, r2, p2)


def _rope_tables(s_len):
    rows = s_len // GRID_W
    row = jnp.repeat(jnp.arange(rows, dtype=F32), GRID_W)
    col = jnp.tile(jnp.arange(GRID_W, dtype=F32), rows)
    n_pair_axis = HEAD_DIM // 4
    inv_freq = ROPE_THETA ** (-jnp.arange(n_pair_axis, dtype=F32) / n_pair_axis)
    ang = jnp.concatenate([row[:, None] * inv_freq, col[:, None] * inv_freq], axis=-1)
    cos, sin = jnp.cos(ang), jnp.sin(ang)
    return jnp.concatenate([cos, cos], axis=-1), jnp.concatenate([-sin, sin], axis=-1)


def _deinterleave_perm():
    half = HEAD_DIM // 2
    return np.concatenate([2 * np.arange(half), 2 * np.arange(half) + 1])


def kernel(x, c, ada_w, ada_b, norm_g, even_w_in, even_w_out, s5_lam_re, s5_lam_im, s5_log_dt, s5_b_re, s5_b_im, s5_c_re, s5_c_im, s5_d, s5_glu_w, s5_glu_b, q_norm_g, k_norm_g, odd_w_in, odd_w_out, sgu_ln_g, sgu_w, sgu_b, peer_w_q, peer_k1, peer_k2, peer_u, peer_v, final_g):
    bsz, s_len, d = x.shape
    depth = ada_w.shape[0]
    mod = _modulation(c, ada_w, ada_b)
    fg = final_g.reshape(1, d)
    n_a = s5_lam_re.shape[2] * S5_GROUP
    n_q = N_Q_HEADS * HEAD_DIM
    n_kv = N_KV_HEADS * HEAD_DIM
    n_steps = max(1, int(math.log2(s_len // S5_CHUNK)))
    cos_t, sin_t = _rope_tables(s_len)
    perm = _deinterleave_perm()
    u_amax = jnp.maximum(jnp.max(jnp.abs(peer_u), axis=(1, 2)), FP8_TINY)
    u_tabs = (peer_u * (FP8_TARGET / u_amax)[:, None, None]).astype(FP8)
    u_unscale = u_amax * (1.0 / FP8_TARGET)
    u8sq = jnp.square(u_tabs.astype(F32))
    u_rownorm = jnp.sqrt(jnp.max(jnp.sum(u8sq, axis=2), axis=1)) * u_unscale
    v_amax = jnp.maximum(jnp.max(jnp.abs(peer_v), axis=(1, 2)), FP8_TINY)
    v_tabs = (peer_v * (FP8_TARGET / v_amax)[:, None, None]).astype(FP8)
    v_unscale = v_amax * (1.0 / FP8_TARGET)

    pending = None
    for layer in range(depth):
        sh1, sc1, g1, sh2, sc2, g2 = [m.reshape(bsz, 1, d) for m in jnp.split(mod[layer], 6, axis=-1)]
        ng1 = norm_g[layer, 0].reshape(1, d)
        ng2 = norm_g[layer, 1].reshape(1, d)
        i = layer // 2
        if layer % 2 == 0:
            if pending is not None:
                x = _resid(*pending, fg, False)
            w_in = even_w_in[i]
            col = np.arange(w_in.shape[1])
            for hd in range(N_Q_HEADS + N_KV_HEADS):
                lo = n_a + hd * HEAD_DIM
                col[lo:lo + HEAD_DIM] = lo + perm
            w_in_p = w_in[:, col].astype(BF16)
            gq = (q_norm_g[i][perm] * (HEAD_DIM ** -0.5)).reshape(1, HEAD_DIM)
            gk = k_norm_g[i][perm].reshape(1, HEAD_DIM)
            ua, q, k, v = _front0(x, ng1, sc1, sh1, w_in_p, gq, gk, cos_t, sin_t, n_a, n_q, n_kv)
            weights = _s5_weights(s5_lam_re[i], s5_lam_im[i], s5_log_dt[i], s5_b_re[i], s5_b_im[i],
                                  s5_c_re[i], s5_c_im[i], s5_d[i], s5_glu_w[i], s5_glu_b[i], n_steps)
            a_out = _s5(ua, weights, n_steps)
            b_out = _attention(q, k, v)
            w_out = even_w_out[i].astype(BF16)
            x1, ht, h8, rs = _mid0(a_out, b_out, w_out[:n_a], w_out[n_a:], x, g1, ng2, sc2, sh2)
        else:
            x1, ht, h8, rs = _odd(*pending, ng1, sc1, sh1, odd_w_in[i].astype(BF16), sgu_ln_g[i],
                                  sgu_w[i].astype(BF16), sgu_b[i], odd_w_out[i].astype(BF16), g1, ng2, sc2, sh2)
        peer_t = _peer_layer(ht, h8, rs, peer_w_q[layer], peer_k1[layer], peer_k2[layer],
                             u_tabs, u_unscale, u_rownorm, v_tabs, v_unscale, layer)
        pending = (x1, peer_t, g2)
    return _resid(*pending, fg, True)
```

```python
import functools
import math

import numpy as np
import jax
import jax.numpy as jnp
from jax import lax
from jax.experimental import pallas as pl
from jax.experimental.pallas import tpu as pltpu

F32 = jnp.float32
BF16 = jnp.bfloat16
FP8 = jnp.float8_e4m3fn
FP8_TARGET = 224.0
FP8_TINY = 1e-30
EPS = 1e-6
HP = lax.Precision.HIGHEST

V7X_VMEM_LIMIT_BYTES = 56 * 1024 * 1024
LANES = 128
SUBLANES = 8

GRID_W = 64
S5_GROUP = 16
S5_STATE = 64
S5_CHUNK = 16
HEAD_DIM = 128
N_Q_HEADS = 8
N_KV_HEADS = 2
GQA_GROUP = N_Q_HEADS // N_KV_HEADS
ROPE_THETA = 10000.0
SGU_CHUNK = 128
SGU_GROUPS = 8
PEER_HEADS = 8
PEER_NKEYS = 128
PEER_TOPK = 16
DT_CLAMP = -1e-4


def _cparams(semantics):
    return pltpu.CompilerParams(dimension_semantics=semantics, vmem_limit_bytes=V7X_VMEM_LIMIT_BYTES)


def _gelu(x):
    return jax.nn.gelu(x)


def _mod_kernel(c_ref, w_ref, b_ref, o_ref):
    c = c_ref[...]
    ca = c * jax.nn.sigmoid(c)
    o_ref[0] = jnp.dot(ca, w_ref[0], preferred_element_type=F32, precision=HP) + b_ref[0]


def _modulation(c, ada_w, ada_b):
    depth, d, n = ada_w.shape
    bsz = c.shape[0]
    cpad = jnp.zeros((SUBLANES, d), F32).at[:bsz].set(c)
    tn = 1024
    out = pl.pallas_call(
        _mod_kernel,
        grid=(depth, n // tn),
        in_specs=[pl.BlockSpec((SUBLANES, d), lambda l, j: (0, 0)),
                  pl.BlockSpec((1, d, tn), lambda l, j: (l, 0, j)),
                  pl.BlockSpec((1, 1, tn), lambda l, j: (l, 0, j))],
        out_specs=pl.BlockSpec((1, SUBLANES, tn), lambda l, j: (l, 0, j)),
        out_shape=jax.ShapeDtypeStruct((depth, SUBLANES, n), F32),
        compiler_params=_cparams(("arbitrary", "arbitrary")),
        name="modulation",
    )(cpad, ada_w, ada_b.reshape(depth, 1, n))
    return out[:, :bsz]


def _ada_norm(x, g, sc, sh):
    ms = jnp.mean(x * x, axis=-1, keepdims=True)
    return x * lax.rsqrt(ms + EPS) * (g * (1.0 + sc)) + sh


def _front0_kernel(x_ref, g_ref, sc_ref, sh_ref, w_ref, gq_ref, gk_ref, cos_ref, sin_ref,
                   ua_ref, q_ref, k_ref, v_ref, *, n_a, n_q, n_kv):
    h = _ada_norm(x_ref[0], g_ref[...], sc_ref[0], sh_ref[0])
    z = jnp.dot(h.astype(BF16), w_ref[...], preferred_element_type=F32)
    ua_ref[0] = z[:, :n_a]
    cos = cos_ref[...]
    sin = sin_ref[...]

    def norm_rope(y, g):
        y = y * lax.rsqrt(jnp.mean(y * y, axis=-1, keepdims=True) + EPS) * g
        return y * cos + pltpu.roll(y, HEAD_DIM // 2, 1) * sin

    for hd in range(n_q // HEAD_DIM):
        lo = n_a + hd * HEAD_DIM
        q_ref[0, :, hd * HEAD_DIM:(hd + 1) * HEAD_DIM] = norm_rope(z[:, lo:lo + HEAD_DIM], gq_ref[...]).astype(BF16)
    for hd in range(n_kv // HEAD_DIM):
        lo = n_a + n_q + hd * HEAD_DIM
        k_ref[0, :, hd * HEAD_DIM:(hd + 1) * HEAD_DIM] = norm_rope(z[:, lo:lo + HEAD_DIM], gk_ref[...]).astype(BF16)
    v_ref[0] = z[:, n_a + n_q + n_kv:].astype(BF16)


def _front0(x, g, sc, sh, w_in, gq, gk, cos_t, sin_t, n_a, n_q, n_kv, tm=256):
    bsz, s_len, d = x.shape
    n_all = w_in.shape[1]
    vec = lambda: pl.BlockSpec((1, 1, d), lambda b, i: (b, 0, 0))
    return pl.pallas_call(
        functools.partial(_front0_kernel, n_a=n_a, n_q=n_q, n_kv=n_kv),
        grid=(bsz, s_len // tm),
        in_specs=[pl.BlockSpec((1, tm, d), lambda b, i: (b, i, 0)),
                  pl.BlockSpec((1, d), lambda b, i: (0, 0)),
                  vec(), vec(),
                  pl.BlockSpec((d, n_all), lambda b, i: (0, 0)),
                  pl.BlockSpec((1, HEAD_DIM), lambda b, i: (0, 0)),
                  pl.BlockSpec((1, HEAD_DIM), lambda b, i: (0, 0)),
                  pl.BlockSpec((tm, HEAD_DIM), lambda b, i: (i, 0)),
                  pl.BlockSpec((tm, HEAD_DIM), lambda b, i: (i, 0))],
        out_specs=[pl.BlockSpec((1, tm, n_a), lambda b, i: (b, i, 0)),
                   pl.BlockSpec((1, tm, n_q), lambda b, i: (b, i, 0)),
                   pl.BlockSpec((1, tm, n_kv), lambda b, i: (b, i, 0)),
                   pl.BlockSpec((1, tm, n_kv), lambda b, i: (b, i, 0))],
        out_shape=[jax.ShapeDtypeStruct((bsz, s_len, n_a), F32),
                   jax.ShapeDtypeStruct((bsz, s_len, n_q), BF16),
                   jax.ShapeDtypeStruct((bsz, s_len, n_kv), BF16),
                   jax.ShapeDtypeStruct((bsz, s_len, n_kv), BF16)],
        compiler_params=_cparams(("arbitrary", "arbitrary")),
        name="front0",
    )(x, g, sc, sh, w_in, gq, gk, cos_t, sin_t)


def _taps_kernel(cr_ref, ci_ref, br_ref, bi_ref, o_ref):
    for j in range(cr_ref.shape[0]):
        o_ref[j] = (jnp.dot(cr_ref[j], br_ref[j], preferred_element_type=F32, precision=HP)
                    - jnp.dot(ci_ref[j], bi_ref[j], preferred_element_type=F32, precision=HP))


def _s5_taps(cp_r, cp_i, bb_r, bb_i, per_step=8):
    n_b, rows, n_p = cp_r.shape
    n_h = bb_r.shape[-1]
    per_step = min(per_step, n_b)
    lhs = pl.BlockSpec((per_step, rows, n_p), lambda i: (i, 0, 0))
    rhs = pl.BlockSpec((per_step, n_p, n_h), lambda i: (i, 0, 0))
    return pl.pallas_call(
        _taps_kernel,
        grid=(n_b // per_step,),
        in_specs=[lhs, lhs, rhs, rhs],
        out_specs=pl.BlockSpec((per_step, rows, n_h), lambda i: (i, 0, 0)),
        out_shape=jax.ShapeDtypeStruct((n_b, rows, n_h), F32),
        compiler_params=_cparams(("arbitrary",)),
        name="s5_taps",
    )(cp_r, cp_i, bb_r, bb_i)


def _s5_weights(lam_re, lam_im, log_dt, b_re, b_im, c_re, c_im, d_skip, glu_w, glu_b, n_steps):
    L = S5_CHUNK
    n_g, n_p = lam_re.shape[1], lam_re.shape[2]
    n_h = b_re.shape[-1]
    lr = jnp.minimum(lam_re.astype(F32), DT_CLAMP)
    li = lam_im.astype(F32)
    dt = jnp.exp(log_dt.astype(F32))[..., None]
    mag = jnp.exp(lr * dt)
    ar = mag * jnp.cos(li * dt)
    ai = mag * jnp.sin(li * dt)
    den = lr * lr + li * li
    nr = ar - 1.0
    cr = (nr * lr + ai * li) / den
    ci = (ai * lr - nr * li) / den
    bbr = cr[..., None] * b_re - ci[..., None] * b_im
    bbi = cr[..., None] * b_im + ci[..., None] * b_re
    pr, pi = [jnp.ones_like(ar)], [jnp.zeros_like(ai)]
    for _ in range(L):
        pr_n = pr[-1] * ar - pi[-1] * ai
        pi_n = pr[-1] * ai + pi[-1] * ar
        pr.append(pr_n)
        pi.append(pi_n)
    pw_r = jnp.stack(pr)
    pw_i = jnp.stack(pi)
    cre = c_re.astype(F32)[None]
    cim = c_im.astype(F32)[None]
    cp_r = cre * pw_r[:, :, :, None, :] - cim * pw_i[:, :, :, None, :]
    cp_i = cre * pw_i[:, :, :, None, :] + cim * pw_r[:, :, :, None, :]
    to_rows = lambda c: c[:L].transpose(1, 2, 0, 3, 4).reshape(2 * n_g, L * n_h, n_p)
    taps = _s5_taps(to_rows(cp_r), to_rows(cp_i), bbr.reshape(2 * n_g, n_p, n_h), bbi.reshape(2 * n_g, n_p, n_h))
    kern = taps.reshape(2, n_g, L, n_h, n_h).transpose(2, 0, 1, 3, 4)
    kf, kb = kern[:, 0], kern[:, 1]
    centre = kf[0] + kb[0] + d_skip.astype(F32)[:, :, None] * jnp.eye(n_h, dtype=F32)
    ktab = jnp.concatenate([kb[1:][::-1], centre[None], kf[1:]], axis=0)
    idx = np.arange(L)[:, None] - np.arange(L)[None, :] + L - 1
    m_full = ktab[idx]
    mt = m_full.transpose(2, 1, 4, 0, 3).reshape(n_g, L * n_h, L * n_h)

    def win_part(d, order):
        wr = pw_r[order, d][:, :, :, None] * bbr[d][None] - pw_i[order, d][:, :, :, None] * bbi[d][None]
        wi = pw_r[order, d][:, :, :, None] * bbi[d][None] + pw_i[order, d][:, :, :, None] * bbr[d][None]
        tr = lambda w: w.transpose(1, 0, 3, 2).reshape(n_g, L * n_h, n_p)
        return tr(wr), tr(wi)

    wf_r, wf_i = win_part(0, np.arange(L - 1, -1, -1))
    wb_r, wb_i = win_part(1, np.arange(L))
    win = jnp.concatenate([wf_r, wf_i, wb_r, wb_i], axis=-1)

    def wout_part(d, order):
        er = cp_r[order, d]
        ei = cp_i[order, d]
        tr = lambda w: w.transpose(1, 3, 0, 2).reshape(n_g, n_p, L * n_h)
        return tr(er), tr(-ei)

    of_r, of_i = wout_part(0, np.arange(1, L + 1))
    ob_r, ob_i = wout_part(1, np.arange(L, 0, -1))
    wout = jnp.concatenate([of_r, of_i, ob_r, ob_i], axis=1)

    glu = jnp.einsum('ts,ghk->gthsk', jnp.eye(L, dtype=F32), glu_w.astype(F32)).reshape(n_g, L * n_h, L * n_h)
    glub = jnp.tile(glu_b.astype(F32), (1, L)).reshape(n_g, 1, L * n_h)

    qr, qi = pw_r[L], pw_i[L]
    rows = []
    for _ in range(n_steps):
        rows.append((jnp.concatenate([qr, qr], -1), jnp.concatenate([-qi, qi], -1)))
        qr, qi = qr * qr - qi * qi, 2.0 * qr * qi
    coef = jnp.stack([jnp.stack([a1, a2], axis=0) for a1, a2 in rows], axis=0)
    coef = coef.transpose(3, 2, 0, 1, 4).reshape(n_g, 2 * n_steps * 2, 2 * n_p)
    return mt.astype(BF16), win.astype(BF16), wout.astype(BF16), glu.astype(BF16), glub, coef


def _s5_kernel(u_ref, perm_ref, mt_ref, win_ref, wout_ref, glu_ref, glub_ref, coef_ref, o_ref, *, n_steps):
    L = S5_CHUNK
    n_c = u_ref.shape[1] // L
    gpb, lh = mt_ref.shape[0], mt_ref.shape[1]
    ww = jnp.concatenate([u_ref.at[0][pl.ds(t, n_c, stride=L), :].astype(BF16) for t in range(L)], axis=1)
    xp = jnp.dot(ww, perm_ref[...], preferred_element_type=F32).astype(BF16)
    row = lax.broadcasted_iota(jnp.int32, (n_c, 2 * S5_STATE), 0)

    def scan(x, g, d):
        for k in range(n_steps):
            sh = 1 << k
            if d == 0:
                shifted = jnp.where(row >= sh, pltpu.roll(x, sh, 0), 0.0)
            else:
                shifted = jnp.where(row < n_c - sh, pltpu.roll(x, n_c - sh, 0), 0.0)
            base = (d * n_steps + k) * 2
            a1 = coef_ref[g, base:base + 1, :]
            a2 = coef_ref[g, base + 1:base + 2, :]
            x = x + a1 * shifted + a2 * pltpu.roll(shifted, S5_STATE, 1)
        if d == 0:
            return jnp.where(row >= 1, pltpu.roll(x, 1, 0), 0.0)
        return jnp.where(row < n_c - 1, pltpu.roll(x, n_c - 1, 0), 0.0)

    outs = []
    for g in range(gpb):
        u = xp[:, g * lh:(g + 1) * lh]
        st = jnp.dot(u, win_ref[g], preferred_element_type=F32)
        two_p = st.shape[1] // 2
        hcat = jnp.concatenate([scan(st[:, :two_p], g, 0), scan(st[:, two_p:], g, 1)], axis=1).astype(BF16)
        y = (jnp.dot(u, mt_ref[g], preferred_element_type=F32)
             + jnp.dot(hcat, wout_ref[g], preferred_element_type=F32))
        y = _gelu(y)
        gate = jax.nn.sigmoid(jnp.dot(y.astype(BF16), glu_ref[g], preferred_element_type=F32) + glub_ref[g])
        outs.append((y * gate).astype(BF16))
    back = lax.dot_general(jnp.concatenate(outs, axis=1), perm_ref[...], (((1,), (1,)), ((), ())),
                           preferred_element_type=F32)
    for t in range(L):
        o_ref.at[0][pl.ds(t, n_c, stride=L), :] = back[:, t * LANES:(t + 1) * LANES]


def _s5(ua, weights, n_steps):
    mt, win, wout, glu, glub, coef = weights
    bsz, s_len, width = ua.shape
    n_g = mt.shape[0]
    n_h = width // n_g
    L = S5_CHUNK
    gpb = LANES // n_h
    n_lb = width // LANES
    cols = L * LANES
    ci = np.arange(cols)
    t_i, g_i, h_i = ci // LANES, (ci % LANES) // n_h, ci % n_h
    perm = (jnp.asarray(g_i * (L * n_h) + t_i * n_h + h_i)[:, None] == jnp.arange(cols)[None, :]).astype(BF16)
    blk = lambda a: pl.BlockSpec((gpb,) + a.shape[1:], lambda b, lb: (lb, 0, 0))
    return pl.pallas_call(
        functools.partial(_s5_kernel, n_steps=n_steps),
        grid=(bsz, n_lb),
        in_specs=[pl.BlockSpec((1, s_len, LANES), lambda b, lb: (b, 0, lb)),
                  pl.BlockSpec((cols, cols), lambda b, lb: (0, 0)),
                  blk(mt), blk(win), blk(wout), blk(glu), blk(glub), blk(coef)],
        out_specs=pl.BlockSpec((1, s_len, LANES), lambda b, lb: (b, 0, lb)),
        out_shape=jax.ShapeDtypeStruct((bsz, s_len, width), F32),
        compiler_params=_cparams(("arbitrary", "arbitrary")),
        name="s5",
    )(ua, perm, mt, win, wout, glu, glub, coef)


def _attn_kernel(q_ref, k_ref, v_ref, o_ref, *, tk):
    tq = q_ref.shape[1]
    qs = jnp.concatenate([q_ref[0, :, h * HEAD_DIM:(h + 1) * HEAD_DIM] for h in range(GQA_GROUP)], axis=0)
    n_rows = qs.shape[0]
    n_kv = k_ref.shape[1] // tk

    def body(j, carry):
        m, l, acc = carry
        start = pl.multiple_of(j * tk, tk)
        kc = k_ref[0, pl.ds(start, tk), :]
        vc = v_ref[0, pl.ds(start, tk), :]
        s = lax.dot_general(qs, kc, (((1,), (1,)), ((), ())), preferred_element_type=F32)
        m_new = jnp.maximum(m, jnp.max(s, axis=-1, keepdims=True))
        p = jnp.exp(s - m_new)
        alpha = jnp.exp(m - m_new)
        l = alpha * l + jnp.sum(p, axis=-1, keepdims=True)
        acc = alpha * acc + jnp.dot(p.astype(BF16), vc, preferred_element_type=F32)
        return m_new, l, acc

    m0 = jnp.full((n_rows, 1), -jnp.inf, F32)
    l0 = jnp.zeros((n_rows, 1), F32)
    a0 = jnp.zeros((n_rows, HEAD_DIM), F32)
    _, l, acc = lax.fori_loop(0, n_kv, body, (m0, l0, a0), unroll=True)
    out = acc / l
    for h in range(GQA_GROUP):
        o_ref[0, :, h * HEAD_DIM:(h + 1) * HEAD_DIM] = out[h * tq:(h + 1) * tq].astype(BF16)


def _attention(q, k, v, tq=128, tk=2048):
    bsz, s_len, n_q = q.shape
    gw = GQA_GROUP * HEAD_DIM
    tk = min(tk, s_len)
    return pl.pallas_call(
        functools.partial(_attn_kernel, tk=tk),
        grid=(bsz, N_KV_HEADS, s_len // tq),
        in_specs=[pl.BlockSpec((1, tq, gw), lambda b, h, i: (b, i, h)),
                  pl.BlockSpec((1, s_len, HEAD_DIM), lambda b, h, i: (b, 0, h)),
                  pl.BlockSpec((1, s_len, HEAD_DIM), lambda b, h, i: (b, 0, h))],
        out_specs=pl.BlockSpec((1, tq, gw), lambda b, h, i: (b, i, h)),
        out_shape=jax.ShapeDtypeStruct((bsz, s_len, n_q), BF16),
        compiler_params=_cparams(("arbitrary", "arbitrary", "arbitrary")),
        name="attention",
    )(q, k, v)


def _store_peer_inputs(h2, ht_ref, h8_ref, rs_ref):
    h2t = h2.T
    ht_ref[...] = h2t.astype(BF16)
    amax = jnp.maximum(jnp.max(jnp.abs(h2t), axis=0, keepdims=True), FP8_TINY)
    h8_ref[...] = (h2t * (FP8_TARGET / amax)).astype(FP8)
    rs_ref[0:1, :] = amax * (1.0 / FP8_TARGET)
    rs_ref[1:2, :] = jnp.sqrt(jnp.sum(h2t * h2t, axis=0, keepdims=True))


def _mid0_kernel(a_ref, b_ref, wa_ref, wb_ref, x_ref, g1_ref, g_ref, sc_ref, sh_ref, x1_ref, ht_ref, h8_ref, rs_ref):
    m = (jnp.dot(a_ref[0].astype(BF16), wa_ref[...], preferred_element_type=F32)
         + jnp.dot(b_ref[0], wb_ref[...], preferred_element_type=F32))
    x1 = x_ref[0] + g1_ref[0] * m
    x1_ref[0] = x1
    h2 = _ada_norm(x1, g_ref[...], sc_ref[0], sh_ref[0])
    _store_peer_inputs(h2, ht_ref, h8_ref, rs_ref)


def _mid0(a_out, b_out, w_a, w_b, x, g1, g, sc, sh, tm=256):
    bsz, s_len, d = x.shape
    n_t = s_len // tm
    wa_n, wb_n = w_a.shape[0], w_b.shape[0]
    vec = lambda: pl.BlockSpec((1, 1, d), lambda b, i: (b, 0, 0))
    return pl.pallas_call(
        _mid0_kernel,
        grid=(bsz, n_t),
        in_specs=[pl.BlockSpec((1, tm, wa_n), lambda b, i: (b, i, 0)),
                  pl.BlockSpec((1, tm, wb_n), lambda b, i: (b, i, 0)),
                  pl.BlockSpec((wa_n, d), lambda b, i: (0, 0)),
                  pl.BlockSpec((wb_n, d), lambda b, i: (0, 0)),
                  pl.BlockSpec((1, tm, d), lambda b, i: (b, i, 0)),
                  vec(),
                  pl.BlockSpec((1, d), lambda b, i: (0, 0)),
                  vec(), vec()],
        out_specs=[pl.BlockSpec((1, tm, d), lambda b, i: (b, i, 0)),
                   pl.BlockSpec((d, tm), lambda b, i: (0, b * n_t + i)),
                   pl.BlockSpec((d, tm), lambda b, i: (0, b * n_t + i)),
                   pl.BlockSpec((2, tm), lambda b, i: (0, b * n_t + i))],
        out_shape=[jax.ShapeDtypeStruct((bsz, s_len, d), F32),
                   jax.ShapeDtypeStruct((d, bsz * s_len), BF16),
                   jax.ShapeDtypeStruct((d, bsz * s_len), FP8),
                   jax.ShapeDtypeStruct((2, bsz * s_len), F32)],
        compiler_params=_cparams(("arbitrary", "arbitrary")),
        name="mid0",
    )(a_out, b_out, w_a, w_b, x, g1, g, sc, sh)


def _odd_kernel(xp_ref, pt_ref, g2p_ref, g_ref, sc_ref, sh_ref, wu_ref, wv_ref, lng_ref, ws_ref, bs_ref, wo_ref,
                g1_ref, g2n_ref, sc2_ref, sh2_ref, x1_ref, ht_ref, h8_ref, rs_ref, h_scr, acc_scr):
    gi = pl.program_id(2)
    n_groups = pl.num_programs(2)

    @pl.when(gi == 0)
    def _():
        x = xp_ref[0] + g2p_ref[0] * pt_ref[...].T
        x1_ref[0] = x
        h_scr[...] = _ada_norm(x, g_ref[...], sc_ref[0], sh_ref[0]).astype(BF16)
        acc_scr[...] = jnp.zeros_like(acc_scr)

    h = h_scr[...]
    zu = _gelu(jnp.dot(h, wu_ref[...], preferred_element_type=F32))
    zv = _gelu(jnp.dot(h, wv_ref[...], preferred_element_type=F32))
    mu = jnp.mean(zv, axis=-1, keepdims=True)
    cen = zv - mu
    var = jnp.mean(cen * cen, axis=-1, keepdims=True)
    vn = (cen * lax.rsqrt(var + EPS) * lng_ref[0]).astype(BF16)
    ws = ws_ref[0]
    bias = bs_ref[0]
    tm = zu.shape[0]
    gated = []
    for c in range(tm // SGU_CHUNK):
        sl = slice(c * SGU_CHUNK, (c + 1) * SGU_CHUNK)
        s = jnp.dot(ws, vn[sl], preferred_element_type=F32) + bias
        gated.append((zu[sl] * s).astype(BF16))
    gated = jnp.concatenate(gated, axis=0)
    acc_scr[...] += jnp.dot(gated, wo_ref[...], preferred_element_type=F32)

    @pl.when(gi == n_groups - 1)
    def _():
        x1 = x1_ref[0] + g1_ref[0] * acc_scr[...]
        x1_ref[0] = x1
        h2 = _ada_norm(x1, g2n_ref[...], sc2_ref[0], sh2_ref[0])
        _store_peer_inputs(h2, ht_ref, h8_ref, rs_ref)


def _odd(xp, peer_t, g2p, g, sc, sh, w_in, ln_g, w_s, b_s, w_out, g1, g2n, sc2, sh2, tm=512):
    bsz, s_len, d = xp.shape
    n_t = s_len // tm
    width = w_out.shape[0]
    gw = width // SGU_GROUPS
    vec = lambda: pl.BlockSpec((1, 1, d), lambda b, i, gi: (b, 0, 0))
    row = lambda: pl.BlockSpec((1, d), lambda b, i, gi: (0, 0))
    bias = jnp.broadcast_to(b_s.astype(F32)[:, :, None], (SGU_GROUPS, SGU_CHUNK, gw))
    return pl.pallas_call(
        _odd_kernel,
        grid=(bsz, n_t, SGU_GROUPS),
        in_specs=[pl.BlockSpec((1, tm, d), lambda b, i, gi: (b, i, 0)),
                  pl.BlockSpec((d, tm), lambda b, i, gi: (0, b * n_t + i)),
                  vec(),
                  row(), vec(), vec(),
                  pl.BlockSpec((d, gw), lambda b, i, gi: (0, gi)),
                  pl.BlockSpec((d, gw), lambda b, i, gi: (0, SGU_GROUPS + gi)),
                  pl.BlockSpec((1, 1, gw), lambda b, i, gi: (gi, 0, 0)),
                  pl.BlockSpec((1, SGU_CHUNK, SGU_CHUNK), lambda b, i, gi: (gi, 0, 0)),
                  pl.BlockSpec((1, SGU_CHUNK, gw), lambda b, i, gi: (gi, 0, 0)),
                  pl.BlockSpec((gw, d), lambda b, i, gi: (gi, 0)),
                  vec(), row(), vec(), vec()],
        out_specs=[pl.BlockSpec((1, tm, d), lambda b, i, gi: (b, i, 0)),
                   pl.BlockSpec((d, tm), lambda b, i, gi: (0, b * n_t + i)),
                   pl.BlockSpec((d, tm), lambda b, i, gi: (0, b * n_t + i)),
                   pl.BlockSpec((2, tm), lambda b, i, gi: (0, b * n_t + i))],
        out_shape=[jax.ShapeDtypeStruct((bsz, s_len, d), F32),
                   jax.ShapeDtypeStruct((d, bsz * s_len), BF16),
                   jax.ShapeDtypeStruct((d, bsz * s_len), FP8),
                   jax.ShapeDtypeStruct((2, bsz * s_len), F32)],
        scratch_shapes=[pltpu.VMEM((tm, d), BF16), pltpu.VMEM((tm, d), F32)],
        compiler_params=_cparams(("arbitrary", "arbitrary", "arbitrary")),
        name="odd_mixer",
    )(xp, peer_t, g2p, g, sc, sh, w_in, w_in, ln_g.reshape(SGU_GROUPS, 1, gw), w_s, bias, w_out, g1, g2n, sc2, sh2)


def _compose_kernel(k_ref, w_ref, o_ref):
    o_ref[0, 0] = lax.dot_general(k_ref[0], w_ref[...], (((1,), (1,)), ((), ())),
                                  preferred_element_type=F32, precision=HP)


def _compose_score_weights(w_q, k1, k2):
    d = w_q.shape[0]
    dk = k1.shape[1]
    keys = jnp.stack([k1, k2]).astype(F32)
    out = pl.pallas_call(
        _compose_kernel,
        grid=(2, PEER_HEADS),
        in_specs=[pl.BlockSpec((1, PEER_NKEYS, dk), lambda a, h: (a, 0, 0)),
                  pl.BlockSpec((d, dk), lambda a, h: (0, h * 2 + a))],
        out_specs=pl.BlockSpec((1, 1, PEER_NKEYS, d), lambda a, h: (a, h, 0, 0)),
        out_shape=jax.ShapeDtypeStruct((2, PEER_HEADS, PEER_NKEYS, d), F32),
        compiler_params=_cparams(("arbitrary", "arbitrary")),
        name="compose_scores",
    )(keys, w_q.astype(F32))
    return out.transpose(0, 2, 1, 3).reshape(2 * PEER_NKEYS * PEER_HEADS, d).astype(BF16)


def _batcher_pairs(n):
    pairs = []
    p = 1
    while p < n:
        k = p
        while k >= 1:
            for j in range(k % p, n - k, 2 * k):
                for i in range(min(k, n - j - k)):
                    if (i + j) // (2 * p) == (i + j + k) // (2 * p):
                        pairs.append((i + j, i + j + k))
            k //= 2
        p *= 2
    return pairs


_SORT16 = _batcher_pairs(PEER_TOPK)


def _vmax(a, b):
    if a is None:
        return b
    if b is None:
        return a
    return jnp.maximum(a, b)


def _vmin(a, b):
    if a is None or b is None:
        return None
    return jnp.minimum(a, b)


def _sort16_desc(v):
    v = list(v)
    for i, j in _SORT16:
        v[i], v[j] = _vmax(v[i], v[j]), _vmin(v[i], v[j])
    return v


def _merge_top16(a, b):
    n = PEER_TOPK
    c = [_vmax(a[i], b[n - 1 - i]) for i in range(n)]
    d = n // 2
    while d >= 1:
        for i in range(n):
            if (i & d) == 0:
                c[i], c[i + d] = _vmax(c[i], c[i + d]), _vmin(c[i], c[i + d])
        d //= 2
    return c


def _top16_sorted(vals):
    groups = [_sort16_desc(vals[i:i + PEER_TOPK]) for i in range(0, len(vals), PEER_TOPK)]
    while len(groups) > 1:
        groups = [_merge_top16(groups[i], groups[i + 1]) for i in range(0, len(groups), 2)]
    return groups[0]


def _route_kernel(ht_ref, w_ref, cnt_ref, p1_ref, r2_ref, p2_ref, r2_scr, p2_scr, *, tw):
    for sp in range(ht_ref.shape[1] // tw):
        _route_block(ht_ref, w_ref, cnt_ref, p1_ref, r2_ref, p2_ref, r2_scr, p2_scr, sp * tw, tw)


def _route_block(ht_ref, w_ref, cnt_ref, p1_ref, r2_ref, p2_ref, r2_scr, p2_scr, lane0, tw):
    nh, nk = PEER_HEADS, PEER_NKEYS
    tok = slice(lane0, lane0 + tw)
    lb0 = lane0 // LANES
    st = jnp.dot(w_ref[...], ht_ref[:, tok], preferred_element_type=F32)
    s1 = [st[i * nh:(i + 1) * nh] for i in range(nk)]
    s2 = [st[(nk + i) * nh:(nk + i + 1) * nh] for i in range(nk)]
    a = _top16_sorted(s1)
    b = _top16_sorted(s2)
    cands = [[a[k] + b[l] for l in range(PEER_TOPK // (k + 1))] for k in range(PEER_TOPK)]
    flat = [c for rowc in cands[1:] for c in rowc]
    flat = flat + [None] * (-len(flat) % PEER_TOPK)
    groups = [cands[0]] + [_sort16_desc(flat[i:i + PEER_TOPK]) for i in range(0, len(flat), PEER_TOPK)]
    top = groups[0]
    for gr in groups[1:]:
        top = _merge_top16(top, gr)
    theta = top[PEER_TOPK - 1]
    z = jnp.ones_like(theta)
    for t in top[1:]:
        z = z + jnp.exp(t - top[0])
    inv_z = 1.0 / z
    n_lb = tw // LANES
    cnt_k = []
    for k in range(PEER_TOPK):
        c = jnp.zeros_like(theta)
        for cand in cands[k]:
            c = c + jnp.where(cand >= theta, 1.0, 0.0)
        cnt_k.append(c)
    for i in range(nk):
        rows = slice(i * nh, (i + 1) * nh)
        cnt = jnp.zeros_like(theta)
        rank = jnp.full_like(theta, float(PEER_TOPK))
        for l in range(PEER_TOPK - 1, -1, -1):
            cnt = jnp.where(s1[i] >= a[l], cnt_k[l], cnt)
            rank = jnp.where(s2[i] >= b[l], float(l), rank)
        cnt_ref[rows, tok] = cnt
        p1_ref[rows, tok] = jnp.exp(s1[i] - a[0]) * inv_z
        p2 = jnp.exp(s2[i] - b[0])
        for lb in range(n_lb):
            lanes = slice(lb * LANES, (lb + 1) * LANES)
            r2_scr[lb0 + lb, rows, :] = rank[:, lanes]
            p2_scr[lb0 + lb, rows, :] = p2[:, lanes]
    for h in range(nh):
        for lb in range(lb0, lb0 + n_lb):
            lanes = slice(lb * LANES, (lb + 1) * LANES)
            r2_ref[h, :, lanes] = r2_scr.at[lb][pl.ds(h, nk, stride=nh), :].astype(BF16)
            p2_ref[h, :, lanes] = p2_scr.at[lb][pl.ds(h, nk, stride=nh), :].astype(BF16)


def _route(ht, w_s, tm=512, tw=256):
    d, t = ht.shape
    nh, nk = PEER_HEADS, PEER_NKEYS
    rows = nh * nk
    tw = min(tw, tm)
    return pl.pallas_call(
        functools.partial(_route_kernel, tw=tw),
        grid=(t // tm,),
        in_specs=[pl.BlockSpec((d, tm), lambda i: (0, i)),
                  pl.BlockSpec((2 * rows, d), lambda i: (0, 0))],
        out_specs=[pl.BlockSpec((rows, tm), lambda i: (0, i)),
                   pl.BlockSpec((rows, tm), lambda i: (0, i)),
                   pl.BlockSpec((nh, nk, tm), lambda i: (0, 0, i)),
                   pl.BlockSpec((nh, nk, tm), lambda i: (0, 0, i))],
        out_shape=[jax.ShapeDtypeStruct((rows, t), F32),
                   jax.ShapeDtypeStruct((rows, t), F32),
                   jax.ShapeDtypeStruct((nh, nk, t), BF16),
                   jax.ShapeDtypeStruct((nh, nk, t), BF16)],
        scratch_shapes=[pltpu.VMEM((tm // LANES, rows, LANES), F32), pltpu.VMEM((tm // LANES, rows, LANES), F32)],
        compiler_params=_cparams(("arbitrary",)),
        name="peer_route",
    )(ht, w_s)


def _peer_kernel(h8_ref, rs_ref, u_ref, v_ref, cnt_ref, p1_ref, r2_ref, p2_ref, *rest, n_split, fused_final):
    if fused_final:
        x1_ref, g2_ref, fg_ref, y_ref, o_ref = rest
    else:
        (o_ref,) = rest
    ei = pl.program_id(1)
    nh, nk = PEER_HEADS, PEER_NKEYS
    te = u_ref.shape[0]
    ts = te // n_split

    @pl.when(ei == 0)
    def _():
        o_ref[...] = jnp.zeros_like(o_ref)

    gates = []
    for ii in range(te // nk):
        key_i = ei * (te // nk) + ii
        gate = None
        for h in range(nh):
            r = key_i * nh + h
            cnt = cnt_ref[pl.ds(r, 1), :].astype(BF16)
            p1 = (p1_ref[pl.ds(r, 1), :] * rs_ref[1:2, :]).astype(BF16)
            p2 = p2_ref[h]
            term = jnp.where(r2_ref[h] < cnt, p2, jnp.zeros_like(p2)) * p1
            gate = term if gate is None else gate + term
        gates.append(gate)

    wts = []
    for sp in range(n_split):
        act = jnp.dot(u_ref[sp * ts:(sp + 1) * ts, :], h8_ref[...], preferred_element_type=F32) * rs_ref[0:1, :]
        gate = jnp.concatenate(gates[sp * (ts // nk):(sp + 1) * (ts // nk)], axis=0)
        wts.append((_gelu(act.astype(BF16)) * gate).astype(FP8))
    wt = jnp.concatenate(wts, axis=0)
    o_ref[...] += lax.dot_general(v_ref[...], wt, (((0,), (0,)), ((), ())), preferred_element_type=F32)

    @pl.when(ei == pl.num_programs(1) - 1)
    def _():
        peer_t = o_ref[...] * rs_ref[2:3, :]
        if fused_final:
            x2 = x1_ref[0] + g2_ref[0] * peer_t.T
            ms = jnp.mean(x2 * x2, axis=-1, keepdims=True)
            y_ref[0] = x2 * lax.rsqrt(ms + EPS) * fg_ref[...]
        else:
            o_ref[...] = peer_t


def _peer(h8, rs, u_tabs, v_tabs, layer, cnt, p1, r2, p2, final=None, tm=512, te=1024, n_split=8):
    d, t = h8.shape
    n_e = u_tabs.shape[1]
    nh, nk = PEER_HEADS, PEER_NKEYS
    rows = nh * nk
    in_specs = [pl.BlockSpec((d, tm), lambda i, e: (0, i)),
                pl.BlockSpec((3, tm), lambda i, e: (0, i)),
                pl.BlockSpec((None, te, d), lambda i, e: (layer, e, 0)),
                pl.BlockSpec((None, te, d), lambda i, e: (layer, e, 0)),
                pl.BlockSpec((rows, tm), lambda i, e: (0, i)),
                pl.BlockSpec((rows, tm), lambda i, e: (0, i)),
                pl.BlockSpec((nh, nk, tm), lambda i, e: (0, 0, i)),
                pl.BlockSpec((nh, nk, tm), lambda i, e: (0, 0, i))]
    args = [h8, rs, u_tabs, v_tabs, cnt, p1, r2, p2]
    if final is None:
        out_specs = pl.BlockSpec((d, tm), lambda i, e: (0, i))
        out_shape = jax.ShapeDtypeStruct((d, t), F32)
        scratch = []
    else:
        x1, g2, final_g = final
        bsz, s_len, _ = x1.shape
        n_t = s_len // tm
        in_specs += [pl.BlockSpec((1, tm, d), lambda i, e: (i // n_t, i % n_t, 0)),
                     pl.BlockSpec((1, 1, d), lambda i, e: (i // n_t, 0, 0)),
                     pl.BlockSpec((1, d), lambda i, e: (0, 0))]
        args += [x1, g2, final_g]
        out_specs = pl.BlockSpec((1, tm, d), lambda i, e: (i // n_t, i % n_t, 0))
        out_shape = jax.ShapeDtypeStruct((bsz, s_len, d), F32)
        scratch = [pltpu.VMEM((d, tm), F32)]
    return pl.pallas_call(
        functools.partial(_peer_kernel, n_split=n_split, fused_final=final is not None),
        grid=(t // tm, n_e // te),
        in_specs=in_specs,
        out_specs=out_specs,
        out_shape=out_shape,
        scratch_shapes=scratch,
        compiler_params=_cparams(("arbitrary", "arbitrary")),
        name="peer_dense",
    )(*args)


def _resid_kernel(x_ref, pt_ref, g2_ref, o_ref):
    o_ref[0] = x_ref[0] + g2_ref[0] * pt_ref[...].T


def _resid(x1, peer_t, g2, tm=256):
    bsz, s_len, d = x1.shape
    n_t = s_len // tm
    return pl.pallas_call(
        _resid_kernel,
        grid=(bsz, n_t),
        in_specs=[pl.BlockSpec((1, tm, d), lambda b, i: (b, i, 0)),
                  pl.BlockSpec((d, tm), lambda b, i: (0, b * n_t + i)),
                  pl.BlockSpec((1, 1, d), lambda b, i: (b, 0, 0))],
        out_specs=pl.BlockSpec((1, tm, d), lambda b, i: (b, i, 0)),
        out_shape=jax.ShapeDtypeStruct((bsz, s_len, d), F32),
        compiler_params=_cparams(("arbitrary", "arbitrary")),
        name="peer_residual",
    )(x1, peer_t, g2)


def _peer_layer(ht, h8, rs, w_q, k1, k2, u_tabs, u_unscale, u_rownorm, v_tabs, v_unscale, layer, final=None):
    w_s = _compose_score_weights(w_q, k1, k2)
    cnt, p1, r2, p2 = _route(ht, w_s)
    bound = jnp.maximum(PEER_HEADS * u_rownorm[layer] * rs[1:2], FP8_TINY)
    w_scale = jnp.exp2(jnp.floor(jnp.log2(FP8_TARGET / bound)))
    scales = jnp.concatenate([rs[0:1] * u_unscale[layer], w_scale, v_unscale[layer] / w_scale], axis=0)
    return _peer(h8, scales, u_tabs, v_tabs, layer, cnt, p1, r2, p2, final)


def _rope_tables(s_len):
    rows = s_len // GRID_W
    row = jnp.repeat(jnp.arange(rows, dtype=F32), GRID_W)
    col = jnp.tile(jnp.arange(GRID_W, dtype=F32), rows)
    n_pair_axis = HEAD_DIM // 4
    inv_freq = ROPE_THETA ** (-jnp.arange(n_pair_axis, dtype=F32) / n_pair_axis)
    ang = jnp.concatenate([row[:, None] * inv_freq, col[:, None] * inv_freq], axis=-1)
    cos, sin = jnp.cos(ang), jnp.sin(ang)
    return jnp.concatenate([cos, cos], axis=-1), jnp.concatenate([-sin, sin], axis=-1)


def _deinterleave_perm():
    half = HEAD_DIM // 2
    return np.concatenate([2 * np.arange(half), 2 * np.arange(half) + 1])


def kernel(x, c, ada_w, ada_b, norm_g, even_w_in, even_w_out, s5_lam_re, s5_lam_im, s5_log_dt, s5_b_re, s5_b_im, s5_c_re, s5_c_im, s5_d, s5_glu_w, s5_glu_b, q_norm_g, k_norm_g, odd_w_in, odd_w_out, sgu_ln_g, sgu_w, sgu_b, peer_w_q, peer_k1, peer_k2, peer_u, peer_v, final_g):
    bsz, s_len, d = x.shape
    depth = ada_w.shape[0]
    mod = _modulation(c, ada_w, ada_b)
    fg = final_g.reshape(1, d)
    n_a = s5_lam_re.shape[2] * S5_GROUP
    n_q = N_Q_HEADS * HEAD_DIM
    n_kv = N_KV_HEADS * HEAD_DIM
    n_steps = max(1, int(math.log2(s_len // S5_CHUNK)))
    cos_t, sin_t = _rope_tables(s_len)
    perm = _deinterleave_perm()
    u_amax = jnp.maximum(jnp.max(jnp.abs(peer_u), axis=(1, 2)), FP8_TINY)
    u_tabs = (peer_u * (FP8_TARGET / u_amax)[:, None, None]).astype(FP8)
    u_unscale = u_amax * (1.0 / FP8_TARGET)
    u8sq = jnp.square(u_tabs.astype(F32))
    u_rownorm = jnp.sqrt(jnp.max(jnp.sum(u8sq, axis=2), axis=1)) * u_unscale
    v_amax = jnp.maximum(jnp.max(jnp.abs(peer_v), axis=(1, 2)), FP8_TINY)
    v_tabs = (peer_v * (FP8_TARGET / v_amax)[:, None, None]).astype(FP8)
    v_unscale = v_amax * (1.0 / FP8_TARGET)

    pending = None
    for layer in range(depth):
        sh1, sc1, g1, sh2, sc2, g2 = [m.reshape(bsz, 1, d) for m in jnp.split(mod[layer], 6, axis=-1)]
        ng1 = norm_g[layer, 0].reshape(1, d)
        ng2 = norm_g[layer, 1].reshape(1, d)
        i = layer // 2
        if layer % 2 == 0:
            if pending is not None:
                x = _resid(*pending)
            w_in = even_w_in[i]
            col = np.arange(w_in.shape[1])
            for hd in range(N_Q_HEADS + N_KV_HEADS):
                lo = n_a + hd * HEAD_DIM
                col[lo:lo + HEAD_DIM] = lo + perm
            w_in_p = w_in[:, col].astype(BF16)
            gq = (q_norm_g[i][perm] * (HEAD_DIM ** -0.5)).reshape(1, HEAD_DIM)
            gk = k_norm_g[i][perm].reshape(1, HEAD_DIM)
            ua, q, k, v = _front0(x, ng1, sc1, sh1, w_in_p, gq, gk, cos_t, sin_t, n_a, n_q, n_kv)
            weights = _s5_weights(s5_lam_re[i], s5_lam_im[i], s5_log_dt[i], s5_b_re[i], s5_b_im[i],
                                  s5_c_re[i], s5_c_im[i], s5_d[i], s5_glu_w[i], s5_glu_b[i], n_steps)
            a_out = _s5(ua, weights, n_steps)
            b_out = _attention(q, k, v)
            w_out = even_w_out[i].astype(BF16)
            x1, ht, h8, rs = _mid0(a_out, b_out, w_out[:n_a], w_out[n_a:], x, g1, ng2, sc2, sh2)
        else:
            x1, ht, h8, rs = _odd(*pending, ng1, sc1, sh1, odd_w_in[i].astype(BF16), sgu_ln_g[i],
                                  sgu_w[i].astype(BF16), sgu_b[i], odd_w_out[i].astype(BF16), g1, ng2, sc2, sh2)
        last = layer == depth - 1
        out = _peer_layer(ht, h8, rs, peer_w_q[layer], peer_k1[layer], peer_k2[layer],
                          u_tabs, u_unscale, u_rownorm, v_tabs, v_unscale, layer, (x1, g2, fg) if last else None)
        pending = None if last else (x1, out, g2)
    return out
```

```python
import functools
import math

import numpy as np
import jax
import jax.numpy as jnp
from jax import lax
from jax.experimental import pallas as pl
from jax.experimental.pallas import tpu as pltpu

F32 = jnp.float32
BF16 = jnp.bfloat16
FP8 = jnp.float8_e4m3fn
FP8_TARGET = 224.0
FP8_TINY = 1e-30
EPS = 1e-6
HP = lax.Precision.HIGHEST

V7X_VMEM_LIMIT_BYTES = 56 * 1024 * 1024
LANES = 128
SUBLANES = 8

GRID_W = 64
S5_GROUP = 16
S5_STATE = 64
S5_CHUNK = 16
HEAD_DIM = 128
N_Q_HEADS = 8
N_KV_HEADS = 2
GQA_GROUP = N_Q_HEADS // N_KV_HEADS
ROPE_THETA = 10000.0
SGU_CHUNK = 128
SGU_GROUPS = 8
PEER_HEADS = 8
PEER_NKEYS = 128
PEER_TOPK = 16
DT_CLAMP = -1e-4


def _cparams(semantics):
    return pltpu.CompilerParams(dimension_semantics=semantics, vmem_limit_bytes=V7X_VMEM_LIMIT_BYTES)


def _gelu(x):
    return jax.nn.gelu(x)


def _mod_kernel(c_ref, w_ref, b_ref, o_ref):
    c = c_ref[...]
    ca = c * jax.nn.sigmoid(c)
    o_ref[0] = jnp.dot(ca, w_ref[0], preferred_element_type=F32, precision=HP) + b_ref[0]


def _modulation(c, ada_w, ada_b):
    depth, d, n = ada_w.shape
    bsz = c.shape[0]
    cpad = jnp.zeros((SUBLANES, d), F32).at[:bsz].set(c)
    tn = 1024
    out = pl.pallas_call(
        _mod_kernel,
        grid=(depth, n // tn),
        in_specs=[pl.BlockSpec((SUBLANES, d), lambda l, j: (0, 0)),
                  pl.BlockSpec((1, d, tn), lambda l, j: (l, 0, j)),
                  pl.BlockSpec((1, 1, tn), lambda l, j: (l, 0, j))],
        out_specs=pl.BlockSpec((1, SUBLANES, tn), lambda l, j: (l, 0, j)),
        out_shape=jax.ShapeDtypeStruct((depth, SUBLANES, n), F32),
        compiler_params=_cparams(("arbitrary", "arbitrary")),
        name="modulation",
    )(cpad, ada_w, ada_b.reshape(depth, 1, n))
    return out[:, :bsz]


def _ada_norm(x, g, sc, sh):
    ms = jnp.mean(x * x, axis=-1, keepdims=True)
    return x * lax.rsqrt(ms + EPS) * (g * (1.0 + sc)) + sh


def _front0_kernel(x_ref, g_ref, sc_ref, sh_ref, w_ref, gq_ref, gk_ref, cos_ref, sin_ref,
                   ua_ref, q_ref, k_ref, v_ref, *, n_a, n_q, n_kv):
    h = _ada_norm(x_ref[0], g_ref[...], sc_ref[0], sh_ref[0])
    z = jnp.dot(h.astype(BF16), w_ref[...], preferred_element_type=F32)
    ua_ref[0] = z[:, :n_a]
    cos = cos_ref[...]
    sin = sin_ref[...]

    def norm_rope(y, g):
        y = y * lax.rsqrt(jnp.mean(y * y, axis=-1, keepdims=True) + EPS) * g
        return y * cos + pltpu.roll(y, HEAD_DIM // 2, 1) * sin

    for hd in range(n_q // HEAD_DIM):
        lo = n_a + hd * HEAD_DIM
        q_ref[0, :, hd * HEAD_DIM:(hd + 1) * HEAD_DIM] = norm_rope(z[:, lo:lo + HEAD_DIM], gq_ref[...]).astype(BF16)
    for hd in range(n_kv // HEAD_DIM):
        lo = n_a + n_q + hd * HEAD_DIM
        k_ref[0, :, hd * HEAD_DIM:(hd + 1) * HEAD_DIM] = norm_rope(z[:, lo:lo + HEAD_DIM], gk_ref[...]).astype(BF16)
    v_ref[0] = z[:, n_a + n_q + n_kv:].astype(BF16)


def _front0(x, g, sc, sh, w_in, gq, gk, cos_t, sin_t, n_a, n_q, n_kv, tm=256):
    bsz, s_len, d = x.shape
    n_all = w_in.shape[1]
    vec = lambda: pl.BlockSpec((1, 1, d), lambda b, i: (b, 0, 0))
    return pl.pallas_call(
        functools.partial(_front0_kernel, n_a=n_a, n_q=n_q, n_kv=n_kv),
        grid=(bsz, s_len // tm),
        in_specs=[pl.BlockSpec((1, tm, d), lambda b, i: (b, i, 0)),
                  pl.BlockSpec((1, d), lambda b, i: (0, 0)),
                  vec(), vec(),
                  pl.BlockSpec((d, n_all), lambda b, i: (0, 0)),
                  pl.BlockSpec((1, HEAD_DIM), lambda b, i: (0, 0)),
                  pl.BlockSpec((1, HEAD_DIM), lambda b, i: (0, 0)),
                  pl.BlockSpec((tm, HEAD_DIM), lambda b, i: (i, 0)),
                  pl.BlockSpec((tm, HEAD_DIM), lambda b, i: (i, 0))],
        out_specs=[pl.BlockSpec((1, tm, n_a), lambda b, i: (b, i, 0)),
                   pl.BlockSpec((1, tm, n_q), lambda b, i: (b, i, 0)),
                   pl.BlockSpec((1, tm, n_kv), lambda b, i: (b, i, 0)),
                   pl.BlockSpec((1, tm, n_kv), lambda b, i: (b, i, 0))],
        out_shape=[jax.ShapeDtypeStruct((bsz, s_len, n_a), F32),
                   jax.ShapeDtypeStruct((bsz, s_len, n_q), BF16),
                   jax.ShapeDtypeStruct((bsz, s_len, n_kv), BF16),
                   jax.ShapeDtypeStruct((bsz, s_len, n_kv), BF16)],
        compiler_params=_cparams(("arbitrary", "arbitrary")),
        name="front0",
    )(x, g, sc, sh, w_in, gq, gk, cos_t, sin_t)


def _taps_kernel(cr_ref, ci_ref, br_ref, bi_ref, o_ref):
    for j in range(cr_ref.shape[0]):
        o_ref[j] = (jnp.dot(cr_ref[j], br_ref[j], preferred_element_type=F32, precision=HP)
                    - jnp.dot(ci_ref[j], bi_ref[j], preferred_element_type=F32, precision=HP))


def _s5_taps(cp_r, cp_i, bb_r, bb_i, per_step=8):
    n_b, rows, n_p = cp_r.shape
    n_h = bb_r.shape[-1]
    per_step = min(per_step, n_b)
    lhs = pl.BlockSpec((per_step, rows, n_p), lambda i: (i, 0, 0))
    rhs = pl.BlockSpec((per_step, n_p, n_h), lambda i: (i, 0, 0))
    return pl.pallas_call(
        _taps_kernel,
        grid=(n_b // per_step,),
        in_specs=[lhs, lhs, rhs, rhs],
        out_specs=pl.BlockSpec((per_step, rows, n_h), lambda i: (i, 0, 0)),
        out_shape=jax.ShapeDtypeStruct((n_b, rows, n_h), F32),
        compiler_params=_cparams(("arbitrary",)),
        name="s5_taps",
    )(cp_r, cp_i, bb_r, bb_i)


def _s5_weights(lam_re, lam_im, log_dt, b_re, b_im, c_re, c_im, d_skip, glu_w, glu_b, n_steps):
    L = S5_CHUNK
    n_g, n_p = lam_re.shape[1], lam_re.shape[2]
    n_h = b_re.shape[-1]
    lr = jnp.minimum(lam_re.astype(F32), DT_CLAMP)
    li = lam_im.astype(F32)
    dt = jnp.exp(log_dt.astype(F32))[..., None]
    mag = jnp.exp(lr * dt)
    ar = mag * jnp.cos(li * dt)
    ai = mag * jnp.sin(li * dt)
    den = lr * lr + li * li
    nr = ar - 1.0
    cr = (nr * lr + ai * li) / den
    ci = (ai * lr - nr * li) / den
    bbr = cr[..., None] * b_re - ci[..., None] * b_im
    bbi = cr[..., None] * b_im + ci[..., None] * b_re
    pr, pi = [jnp.ones_like(ar)], [jnp.zeros_like(ai)]
    for _ in range(L):
        pr_n = pr[-1] * ar - pi[-1] * ai
        pi_n = pr[-1] * ai + pi[-1] * ar
        pr.append(pr_n)
        pi.append(pi_n)
    pw_r = jnp.stack(pr)
    pw_i = jnp.stack(pi)
    cre = c_re.astype(F32)[None]
    cim = c_im.astype(F32)[None]
    cp_r = cre * pw_r[:, :, :, None, :] - cim * pw_i[:, :, :, None, :]
    cp_i = cre * pw_i[:, :, :, None, :] + cim * pw_r[:, :, :, None, :]
    to_rows = lambda c: c[:L].transpose(1, 2, 0, 3, 4).reshape(2 * n_g, L * n_h, n_p)
    taps = _s5_taps(to_rows(cp_r), to_rows(cp_i), bbr.reshape(2 * n_g, n_p, n_h), bbi.reshape(2 * n_g, n_p, n_h))
    kern = taps.reshape(2, n_g, L, n_h, n_h).transpose(2, 0, 1, 3, 4)
    kf, kb = kern[:, 0], kern[:, 1]
    centre = kf[0] + kb[0] + d_skip.astype(F32)[:, :, None] * jnp.eye(n_h, dtype=F32)
    ktab = jnp.concatenate([kb[1:][::-1], centre[None], kf[1:]], axis=0)
    idx = np.arange(L)[:, None] - np.arange(L)[None, :] + L - 1
    m_full = ktab[idx]
    mt = m_full.transpose(2, 1, 4, 0, 3).reshape(n_g, L * n_h, L * n_h)

    def win_part(d, order):
        wr = pw_r[order, d][:, :, :, None] * bbr[d][None] - pw_i[order, d][:, :, :, None] * bbi[d][None]
        wi = pw_r[order, d][:, :, :, None] * bbi[d][None] + pw_i[order, d][:, :, :, None] * bbr[d][None]
        tr = lambda w: w.transpose(1, 0, 3, 2).reshape(n_g, L * n_h, n_p)
        return tr(wr), tr(wi)

    wf_r, wf_i = win_part(0, np.arange(L - 1, -1, -1))
    wb_r, wb_i = win_part(1, np.arange(L))
    win = jnp.concatenate([wf_r, wf_i, wb_r, wb_i], axis=-1)

    def wout_part(d, order):
        er = cp_r[order, d]
        ei = cp_i[order, d]
        tr = lambda w: w.transpose(1, 3, 0, 2).reshape(n_g, n_p, L * n_h)
        return tr(er), tr(-ei)

    of_r, of_i = wout_part(0, np.arange(1, L + 1))
    ob_r, ob_i = wout_part(1, np.arange(L, 0, -1))
    wout = jnp.concatenate([of_r, of_i, ob_r, ob_i], axis=1)

    glu = jnp.einsum('ts,ghk->gthsk', jnp.eye(L, dtype=F32), glu_w.astype(F32)).reshape(n_g, L * n_h, L * n_h)
    glub = jnp.tile(glu_b.astype(F32), (1, L)).reshape(n_g, 1, L * n_h)

    qr, qi = pw_r[L], pw_i[L]
    rows = []
    for _ in range(n_steps):
        rows.append((jnp.concatenate([qr, qr], -1), jnp.concatenate([-qi, qi], -1)))
        qr, qi = qr * qr - qi * qi, 2.0 * qr * qi
    coef = jnp.stack([jnp.stack([a1, a2], axis=0) for a1, a2 in rows], axis=0)
    coef = coef.transpose(3, 2, 0, 1, 4).reshape(n_g, 2 * n_steps * 2, 2 * n_p)
    return mt.astype(BF16), win.astype(BF16), wout.astype(BF16), glu.astype(BF16), glub, coef


def _s5_kernel(u_ref, perm_ref, mt_ref, win_ref, wout_ref, glu_ref, glub_ref, coef_ref, o_ref, *, n_steps):
    L = S5_CHUNK
    n_c = u_ref.shape[1] // L
    gpb, lh = mt_ref.shape[0], mt_ref.shape[1]
    ww = jnp.concatenate([u_ref.at[0][pl.ds(t, n_c, stride=L), :].astype(BF16) for t in range(L)], axis=1)
    xp = jnp.dot(ww, perm_ref[...], preferred_element_type=F32).astype(BF16)
    row = lax.broadcasted_iota(jnp.int32, (n_c, 2 * S5_STATE), 0)

    def scan(x, g, d):
        for k in range(n_steps):
            sh = 1 << k
            if d == 0:
                shifted = jnp.where(row >= sh, pltpu.roll(x, sh, 0), 0.0)
            else:
                shifted = jnp.where(row < n_c - sh, pltpu.roll(x, n_c - sh, 0), 0.0)
            base = (d * n_steps + k) * 2
            a1 = coef_ref[g, base:base + 1, :]
            a2 = coef_ref[g, base + 1:base + 2, :]
            x = x + a1 * shifted + a2 * pltpu.roll(shifted, S5_STATE, 1)
        if d == 0:
            return jnp.where(row >= 1, pltpu.roll(x, 1, 0), 0.0)
        return jnp.where(row < n_c - 1, pltpu.roll(x, n_c - 1, 0), 0.0)

    outs = []
    for g in range(gpb):
        u = xp[:, g * lh:(g + 1) * lh]
        st = jnp.dot(u, win_ref[g], preferred_element_type=F32)
        two_p = st.shape[1] // 2
        hcat = jnp.concatenate([scan(st[:, :two_p], g, 0), scan(st[:, two_p:], g, 1)], axis=1).astype(BF16)
        y = (jnp.dot(u, mt_ref[g], preferred_element_type=F32)
             + jnp.dot(hcat, wout_ref[g], preferred_element_type=F32))
        y = _gelu(y)
        gate = jax.nn.sigmoid(jnp.dot(y.astype(BF16), glu_ref[g], preferred_element_type=F32) + glub_ref[g])
        outs.append((y * gate).astype(BF16))
    back = lax.dot_general(jnp.concatenate(outs, axis=1), perm_ref[...], (((1,), (1,)), ((), ())),
                           preferred_element_type=F32)
    for t in range(L):
        o_ref.at[0][pl.ds(t, n_c, stride=L), :] = back[:, t * LANES:(t + 1) * LANES]


def _s5(ua, weights, n_steps):
    mt, win, wout, glu, glub, coef = weights
    bsz, s_len, width = ua.shape
    n_g = mt.shape[0]
    n_h = width // n_g
    L = S5_CHUNK
    gpb = LANES // n_h
    n_lb = width // LANES
    cols = L * LANES
    ci = np.arange(cols)
    t_i, g_i, h_i = ci // LANES, (ci % LANES) // n_h, ci % n_h
    perm = (jnp.asarray(g_i * (L * n_h) + t_i * n_h + h_i)[:, None] == jnp.arange(cols)[None, :]).astype(BF16)
    blk = lambda a: pl.BlockSpec((gpb,) + a.shape[1:], lambda b, lb: (lb, 0, 0))
    return pl.pallas_call(
        functools.partial(_s5_kernel, n_steps=n_steps),
        grid=(bsz, n_lb),
        in_specs=[pl.BlockSpec((1, s_len, LANES), lambda b, lb: (b, 0, lb)),
                  pl.BlockSpec((cols, cols), lambda b, lb: (0, 0)),
                  blk(mt), blk(win), blk(wout), blk(glu), blk(glub), blk(coef)],
        out_specs=pl.BlockSpec((1, s_len, LANES), lambda b, lb: (b, 0, lb)),
        out_shape=jax.ShapeDtypeStruct((bsz, s_len, width), F32),
        compiler_params=_cparams(("arbitrary", "arbitrary")),
        name="s5",
    )(ua, perm, mt, win, wout, glu, glub, coef)


def _attn_kernel(q_ref, k_ref, v_ref, o_ref, *, tk):
    tq = q_ref.shape[1]
    qs = jnp.concatenate([q_ref[0, :, h * HEAD_DIM:(h + 1) * HEAD_DIM] for h in range(GQA_GROUP)], axis=0)
    n_rows = qs.shape[0]
    n_kv = k_ref.shape[1] // tk

    def body(j, carry):
        m, l, acc = carry
        start = pl.multiple_of(j * tk, tk)
        kc = k_ref[0, pl.ds(start, tk), :]
        vc = v_ref[0, pl.ds(start, tk), :]
        s = lax.dot_general(qs, kc, (((1,), (1,)), ((), ())), preferred_element_type=F32)
        m_new = jnp.maximum(m, jnp.max(s, axis=-1, keepdims=True))
        p = jnp.exp(s - m_new)
        alpha = jnp.exp(m - m_new)
        l = alpha * l + jnp.sum(p, axis=-1, keepdims=True)
        acc = alpha * acc + jnp.dot(p.astype(BF16), vc, preferred_element_type=F32)
        return m_new, l, acc

    m0 = jnp.full((n_rows, 1), -jnp.inf, F32)
    l0 = jnp.zeros((n_rows, 1), F32)
    a0 = jnp.zeros((n_rows, HEAD_DIM), F32)
    _, l, acc = lax.fori_loop(0, n_kv, body, (m0, l0, a0), unroll=True)
    out = acc / l
    for h in range(GQA_GROUP):
        o_ref[0, :, h * HEAD_DIM:(h + 1) * HEAD_DIM] = out[h * tq:(h + 1) * tq].astype(BF16)


def _attention(q, k, v, tq=128, tk=2048):
    bsz, s_len, n_q = q.shape
    gw = GQA_GROUP * HEAD_DIM
    tk = min(tk, s_len)
    return pl.pallas_call(
        functools.partial(_attn_kernel, tk=tk),
        grid=(bsz, N_KV_HEADS, s_len // tq),
        in_specs=[pl.BlockSpec((1, tq, gw), lambda b, h, i: (b, i, h)),
                  pl.BlockSpec((1, s_len, HEAD_DIM), lambda b, h, i: (b, 0, h)),
                  pl.BlockSpec((1, s_len, HEAD_DIM), lambda b, h, i: (b, 0, h))],
        out_specs=pl.BlockSpec((1, tq, gw), lambda b, h, i: (b, i, h)),
        out_shape=jax.ShapeDtypeStruct((bsz, s_len, n_q), BF16),
        compiler_params=_cparams(("arbitrary", "arbitrary", "arbitrary")),
        name="attention",
    )(q, k, v)


def _store_peer_inputs(h2, ht_ref, h8_ref, rs_ref):
    h2t = h2.T
    ht_ref[...] = h2t.astype(BF16)
    amax = jnp.maximum(jnp.max(jnp.abs(h2t), axis=0, keepdims=True), FP8_TINY)
    h8_ref[...] = (h2t * (FP8_TARGET / amax)).astype(FP8)
    rs_ref[0:1, :] = amax * (1.0 / FP8_TARGET)
    rs_ref[1:2, :] = jnp.sqrt(jnp.sum(h2t * h2t, axis=0, keepdims=True))


def _mid0_kernel(a_ref, b_ref, wa_ref, wb_ref, x_ref, g1_ref, g_ref, sc_ref, sh_ref, x1_ref, ht_ref, h8_ref, rs_ref):
    m = (jnp.dot(a_ref[0].astype(BF16), wa_ref[...], preferred_element_type=F32)
         + jnp.dot(b_ref[0], wb_ref[...], preferred_element_type=F32))
    x1 = x_ref[0] + g1_ref[0] * m
    x1_ref[0] = x1
    h2 = _ada_norm(x1, g_ref[...], sc_ref[0], sh_ref[0])
    _store_peer_inputs(h2, ht_ref, h8_ref, rs_ref)


def _mid0(a_out, b_out, w_a, w_b, x, g1, g, sc, sh, tm=256):
    bsz, s_len, d = x.shape
    n_t = s_len // tm
    wa_n, wb_n = w_a.shape[0], w_b.shape[0]
    vec = lambda: pl.BlockSpec((1, 1, d), lambda b, i: (b, 0, 0))
    return pl.pallas_call(
        _mid0_kernel,
        grid=(bsz, n_t),
        in_specs=[pl.BlockSpec((1, tm, wa_n), lambda b, i: (b, i, 0)),
                  pl.BlockSpec((1, tm, wb_n), lambda b, i: (b, i, 0)),
                  pl.BlockSpec((wa_n, d), lambda b, i: (0, 0)),
                  pl.BlockSpec((wb_n, d), lambda b, i: (0, 0)),
                  pl.BlockSpec((1, tm, d), lambda b, i: (b, i, 0)),
                  vec(),
                  pl.BlockSpec((1, d), lambda b, i: (0, 0)),
                  vec(), vec()],
        out_specs=[pl.BlockSpec((1, tm, d), lambda b, i: (b, i, 0)),
                   pl.BlockSpec((d, tm), lambda b, i: (0, b * n_t + i)),
                   pl.BlockSpec((d, tm), lambda b, i: (0, b * n_t + i)),
                   pl.BlockSpec((2, tm), lambda b, i: (0, b * n_t + i))],
        out_shape=[jax.ShapeDtypeStruct((bsz, s_len, d), F32),
                   jax.ShapeDtypeStruct((d, bsz * s_len), BF16),
                   jax.ShapeDtypeStruct((d, bsz * s_len), FP8),
                   jax.ShapeDtypeStruct((2, bsz * s_len), F32)],
        compiler_params=_cparams(("arbitrary", "arbitrary")),
        name="mid0",
    )(a_out, b_out, w_a, w_b, x, g1, g, sc, sh)


def _odd_kernel(xp_ref, pt_ref, g2p_ref, g_ref, sc_ref, sh_ref, wu_ref, wv_ref, lng_ref, ws_ref, bs_ref, wo_ref,
                g1_ref, g2n_ref, sc2_ref, sh2_ref, x1_ref, ht_ref, h8_ref, rs_ref, h_scr, acc_scr):
    gi = pl.program_id(2)
    n_groups = pl.num_programs(2)

    @pl.when(gi == 0)
    def _():
        x = xp_ref[0] + g2p_ref[0] * pt_ref[...].T
        x1_ref[0] = x
        h_scr[...] = _ada_norm(x, g_ref[...], sc_ref[0], sh_ref[0]).astype(BF16)
        acc_scr[...] = jnp.zeros_like(acc_scr)

    h = h_scr[...]
    zu = _gelu(jnp.dot(h, wu_ref[...], preferred_element_type=F32))
    zv = _gelu(jnp.dot(h, wv_ref[...], preferred_element_type=F32))
    mu = jnp.mean(zv, axis=-1, keepdims=True)
    cen = zv - mu
    var = jnp.mean(cen * cen, axis=-1, keepdims=True)
    vn = (cen * lax.rsqrt(var + EPS) * lng_ref[0]).astype(BF16)
    ws = ws_ref[0]
    bias = bs_ref[0]
    tm = zu.shape[0]
    gated = []
    for c in range(tm // SGU_CHUNK):
        sl = slice(c * SGU_CHUNK, (c + 1) * SGU_CHUNK)
        s = jnp.dot(ws, vn[sl], preferred_element_type=F32) + bias
        gated.append((zu[sl] * s).astype(BF16))
    gated = jnp.concatenate(gated, axis=0)
    acc_scr[...] += jnp.dot(gated, wo_ref[...], preferred_element_type=F32)

    @pl.when(gi == n_groups - 1)
    def _():
        x1 = x1_ref[0] + g1_ref[0] * acc_scr[...]
        x1_ref[0] = x1
        h2 = _ada_norm(x1, g2n_ref[...], sc2_ref[0], sh2_ref[0])
        _store_peer_inputs(h2, ht_ref, h8_ref, rs_ref)


def _odd(xp, peer_t, g2p, g, sc, sh, w_in, ln_g, w_s, b_s, w_out, g1, g2n, sc2, sh2, tm=512):
    bsz, s_len, d = xp.shape
    n_t = s_len // tm
    width = w_out.shape[0]
    gw = width // SGU_GROUPS
    vec = lambda: pl.BlockSpec((1, 1, d), lambda b, i, gi: (b, 0, 0))
    row = lambda: pl.BlockSpec((1, d), lambda b, i, gi: (0, 0))
    bias = jnp.broadcast_to(b_s.astype(F32)[:, :, None], (SGU_GROUPS, SGU_CHUNK, gw))
    return pl.pallas_call(
        _odd_kernel,
        grid=(bsz, n_t, SGU_GROUPS),
        in_specs=[pl.BlockSpec((1, tm, d), lambda b, i, gi: (b, i, 0)),
                  pl.BlockSpec((d, tm), lambda b, i, gi: (0, b * n_t + i)),
                  vec(),
                  row(), vec(), vec(),
                  pl.BlockSpec((d, gw), lambda b, i, gi: (0, gi)),
                  pl.BlockSpec((d, gw), lambda b, i, gi: (0, SGU_GROUPS + gi)),
                  pl.BlockSpec((1, 1, gw), lambda b, i, gi: (gi, 0, 0)),
                  pl.BlockSpec((1, SGU_CHUNK, SGU_CHUNK), lambda b, i, gi: (gi, 0, 0)),
                  pl.BlockSpec((1, SGU_CHUNK, gw), lambda b, i, gi: (gi, 0, 0)),
                  pl.BlockSpec((gw, d), lambda b, i, gi: (gi, 0)),
                  vec(), row(), vec(), vec()],
        out_specs=[pl.BlockSpec((1, tm, d), lambda b, i, gi: (b, i, 0)),
                   pl.BlockSpec((d, tm), lambda b, i, gi: (0, b * n_t + i)),
                   pl.BlockSpec((d, tm), lambda b, i, gi: (0, b * n_t + i)),
                   pl.BlockSpec((2, tm), lambda b, i, gi: (0, b * n_t + i))],
        out_shape=[jax.ShapeDtypeStruct((bsz, s_len, d), F32),
                   jax.ShapeDtypeStruct((d, bsz * s_len), BF16),
                   jax.ShapeDtypeStruct((d, bsz * s_len), FP8),
                   jax.ShapeDtypeStruct((2, bsz * s_len), F32)],
        scratch_shapes=[pltpu.VMEM((tm, d), BF16), pltpu.VMEM((tm, d), F32)],
        compiler_params=_cparams(("arbitrary", "arbitrary", "arbitrary")),
        name="odd_mixer",
    )(xp, peer_t, g2p, g, sc, sh, w_in, w_in, ln_g.reshape(SGU_GROUPS, 1, gw), w_s, bias, w_out, g1, g2n, sc2, sh2)


def _compose_kernel(k_ref, w_ref, o_ref):
    o_ref[0, 0] = lax.dot_general(k_ref[0], w_ref[...], (((1,), (1,)), ((), ())),
                                  preferred_element_type=F32, precision=HP)


def _compose_score_weights(w_q, k1, k2):
    d = w_q.shape[0]
    dk = k1.shape[1]
    keys = jnp.stack([k1, k2]).astype(F32)
    out = pl.pallas_call(
        _compose_kernel,
        grid=(2, PEER_HEADS),
        in_specs=[pl.BlockSpec((1, PEER_NKEYS, dk), lambda a, h: (a, 0, 0)),
                  pl.BlockSpec((d, dk), lambda a, h: (0, h * 2 + a))],
        out_specs=pl.BlockSpec((1, 1, PEER_NKEYS, d), lambda a, h: (a, h, 0, 0)),
        out_shape=jax.ShapeDtypeStruct((2, PEER_HEADS, PEER_NKEYS, d), F32),
        compiler_params=_cparams(("arbitrary", "arbitrary")),
        name="compose_scores",
    )(keys, w_q.astype(F32))
    return out.transpose(0, 2, 1, 3).reshape(2 * PEER_NKEYS * PEER_HEADS, d).astype(BF16)


def _batcher_pairs(n):
    pairs = []
    p = 1
    while p < n:
        k = p
        while k >= 1:
            for j in range(k % p, n - k, 2 * k):
                for i in range(min(k, n - j - k)):
                    if (i + j) // (2 * p) == (i + j + k) // (2 * p):
                        pairs.append((i + j, i + j + k))
            k //= 2
        p *= 2
    return pairs


_SORT16 = _batcher_pairs(PEER_TOPK)


def _vmax(a, b):
    if a is None:
        return b
    if b is None:
        return a
    return jnp.maximum(a, b)


def _vmin(a, b):
    if a is None or b is None:
        return None
    return jnp.minimum(a, b)


def _sort16_desc(v):
    v = list(v)
    for i, j in _SORT16:
        v[i], v[j] = _vmax(v[i], v[j]), _vmin(v[i], v[j])
    return v


def _merge_top16(a, b):
    n = PEER_TOPK
    c = [_vmax(a[i], b[n - 1 - i]) for i in range(n)]
    d = n // 2
    while d >= 1:
        for i in range(n):
            if (i & d) == 0:
                c[i], c[i + d] = _vmax(c[i], c[i + d]), _vmin(c[i], c[i + d])
        d //= 2
    return c


def _top16_sorted(vals):
    groups = [_sort16_desc(vals[i:i + PEER_TOPK]) for i in range(0, len(vals), PEER_TOPK)]
    while len(groups) > 1:
        groups = [_merge_top16(groups[i], groups[i + 1]) for i in range(0, len(groups), 2)]
    return groups[0]


def _route_kernel(ht_ref, w_ref, cnt_ref, p1_ref, r2_ref, p2_ref, r2_scr, p2_scr, *, tw):
    for sp in range(ht_ref.shape[1] // tw):
        _route_block(ht_ref, w_ref, cnt_ref, p1_ref, r2_ref, p2_ref, r2_scr, p2_scr, sp * tw, tw)


def _route_block(ht_ref, w_ref, cnt_ref, p1_ref, r2_ref, p2_ref, r2_scr, p2_scr, lane0, tw):
    nh, nk = PEER_HEADS, PEER_NKEYS
    tok = slice(lane0, lane0 + tw)
    lb0 = lane0 // LANES
    st = jnp.dot(w_ref[...], ht_ref[:, tok], preferred_element_type=F32)
    s1 = [st[i * nh:(i + 1) * nh] for i in range(nk)]
    s2 = [st[(nk + i) * nh:(nk + i + 1) * nh] for i in range(nk)]
    a = _top16_sorted(s1)
    b = _top16_sorted(s2)
    cands = [[a[k] + b[l] for l in range(PEER_TOPK // (k + 1))] for k in range(PEER_TOPK)]
    flat = [c for rowc in cands[1:] for c in rowc]
    flat = flat + [None] * (-len(flat) % PEER_TOPK)
    groups = [cands[0]] + [_sort16_desc(flat[i:i + PEER_TOPK]) for i in range(0, len(flat), PEER_TOPK)]
    top = groups[0]
    for gr in groups[1:]:
        top = _merge_top16(top, gr)
    theta = top[PEER_TOPK - 1]
    z = jnp.ones_like(theta)
    for t in top[1:]:
        z = z + jnp.exp(t - top[0])
    inv_z = 1.0 / z
    n_lb = tw // LANES
    cnt_k = []
    for k in range(PEER_TOPK):
        c = jnp.zeros_like(theta)
        for cand in cands[k]:
            c = c + jnp.where(cand >= theta, 1.0, 0.0)
        cnt_k.append(c)
    for i in range(nk):
        rows = slice(i * nh, (i + 1) * nh)
        cnt = jnp.zeros_like(theta)
        rank = jnp.full_like(theta, float(PEER_TOPK))
        for l in range(PEER_TOPK - 1, -1, -1):
            cnt = jnp.where(s1[i] >= a[l], cnt_k[l], cnt)
            rank = jnp.where(s2[i] >= b[l], float(l), rank)
        cnt_ref[rows, tok] = cnt
        p1_ref[rows, tok] = jnp.exp(s1[i] - a[0]) * inv_z
        p2 = jnp.exp(s2[i] - b[0])
        for lb in range(n_lb):
            lanes = slice(lb * LANES, (lb + 1) * LANES)
            r2_scr[lb0 + lb, rows, :] = rank[:, lanes]
            p2_scr[lb0 + lb, rows, :] = p2[:, lanes]
    for h in range(nh):
        for lb in range(lb0, lb0 + n_lb):
            lanes = slice(lb * LANES, (lb + 1) * LANES)
            r2_ref[h, :, lanes] = r2_scr.at[lb][pl.ds(h, nk, stride=nh), :].astype(BF16)
            p2_ref[h, :, lanes] = p2_scr.at[lb][pl.ds(h, nk, stride=nh), :].astype(BF16)


def _route(ht, w_s, tm=512, tw=256):
    d, t = ht.shape
    nh, nk = PEER_HEADS, PEER_NKEYS
    rows = nh * nk
    tw = min(tw, tm)
    return pl.pallas_call(
        functools.partial(_route_kernel, tw=tw),
        grid=(t // tm,),
        in_specs=[pl.BlockSpec((d, tm), lambda i: (0, i)),
                  pl.BlockSpec((2 * rows, d), lambda i: (0, 0))],
        out_specs=[pl.BlockSpec((rows, tm), lambda i: (0, i)),
                   pl.BlockSpec((rows, tm), lambda i: (0, i)),
                   pl.BlockSpec((nh, nk, tm), lambda i: (0, 0, i)),
                   pl.BlockSpec((nh, nk, tm), lambda i: (0, 0, i))],
        out_shape=[jax.ShapeDtypeStruct((rows, t), F32),
                   jax.ShapeDtypeStruct((rows, t), F32),
                   jax.ShapeDtypeStruct((nh, nk, t), BF16),
                   jax.ShapeDtypeStruct((nh, nk, t), BF16)],
        scratch_shapes=[pltpu.VMEM((tm // LANES, rows, LANES), F32), pltpu.VMEM((tm // LANES, rows, LANES), F32)],
        compiler_params=_cparams(("arbitrary",)),
        name="peer_route",
    )(ht, w_s)


def _peer_kernel(h8_ref, rs_ref, u_ref, v_ref, cnt_ref, p1_ref, r2_ref, p2_ref, *rest, n_split, fused_final):
    if fused_final:
        x1_ref, g2_ref, fg_ref, y_ref, o_ref = rest
    else:
        (o_ref,) = rest
    ei = pl.program_id(1)
    nh, nk = PEER_HEADS, PEER_NKEYS
    te = u_ref.shape[0]
    ts = te // n_split

    @pl.when(ei == 0)
    def _():
        o_ref[...] = jnp.zeros_like(o_ref)

    gates = []
    for ii in range(te // nk):
        key_i = ei * (te // nk) + ii
        gate = None
        for h in range(nh):
            r = key_i * nh + h
            cnt = cnt_ref[pl.ds(r, 1), :].astype(BF16)
            p1 = (p1_ref[pl.ds(r, 1), :] * rs_ref[1:2, :]).astype(BF16)
            p2 = p2_ref[h]
            term = jnp.where(r2_ref[h] < cnt, p2, jnp.zeros_like(p2)) * p1
            gate = term if gate is None else gate + term
        gates.append(gate)

    wts = []
    for sp in range(n_split):
        act = jnp.dot(u_ref[sp * ts:(sp + 1) * ts, :], h8_ref[...], preferred_element_type=F32) * rs_ref[0:1, :]
        gate = jnp.concatenate(gates[sp * (ts // nk):(sp + 1) * (ts // nk)], axis=0)
        wts.append((_gelu(act.astype(BF16)) * gate).astype(FP8))
    wt = jnp.concatenate(wts, axis=0)
    o_ref[...] += lax.dot_general(v_ref[...], wt, (((0,), (0,)), ((), ())), preferred_element_type=F32)

    @pl.when(ei == pl.num_programs(1) - 1)
    def _():
        peer_t = o_ref[...] * rs_ref[2:3, :]
        if fused_final:
            x2 = x1_ref[0] + g2_ref[0] * peer_t.T
            ms = jnp.mean(x2 * x2, axis=-1, keepdims=True)
            y_ref[0] = x2 * lax.rsqrt(ms + EPS) * fg_ref[...]
        else:
            o_ref[...] = peer_t


def _peer(h8, rs, u_tabs, v_tabs, layer, cnt, p1, r2, p2, final=None, tm=512, te=None):
    d, t = h8.shape
    n_e = u_tabs.shape[1]
    if te is None:
        te = 2048 if final is None else 1024
    te = min(te, n_e)
    n_split = te // PEER_NKEYS
    nh, nk = PEER_HEADS, PEER_NKEYS
    rows = nh * nk
    in_specs = [pl.BlockSpec((d, tm), lambda i, e: (0, i)),
                pl.BlockSpec((3, tm), lambda i, e: (0, i)),
                pl.BlockSpec((None, te, d), lambda i, e: (layer, e, 0)),
                pl.BlockSpec((None, te, d), lambda i, e: (layer, e, 0)),
                pl.BlockSpec((rows, tm), lambda i, e: (0, i)),
                pl.BlockSpec((rows, tm), lambda i, e: (0, i)),
                pl.BlockSpec((nh, nk, tm), lambda i, e: (0, 0, i)),
                pl.BlockSpec((nh, nk, tm), lambda i, e: (0, 0, i))]
    args = [h8, rs, u_tabs, v_tabs, cnt, p1, r2, p2]
    if final is None:
        out_specs = pl.BlockSpec((d, tm), lambda i, e: (0, i))
        out_shape = jax.ShapeDtypeStruct((d, t), F32)
        scratch = []
    else:
        x1, g2, final_g = final
        bsz, s_len, _ = x1.shape
        n_t = s_len // tm
        in_specs += [pl.BlockSpec((1, tm, d), lambda i, e: (i // n_t, i % n_t, 0)),
                     pl.BlockSpec((1, 1, d), lambda i, e: (i // n_t, 0, 0)),
                     pl.BlockSpec((1, d), lambda i, e: (0, 0))]
        args += [x1, g2, final_g]
        out_specs = pl.BlockSpec((1, tm, d), lambda i, e: (i // n_t, i % n_t, 0))
        out_shape = jax.ShapeDtypeStruct((bsz, s_len, d), F32)
        scratch = [pltpu.VMEM((d, tm), F32)]
    return pl.pallas_call(
        functools.partial(_peer_kernel, n_split=n_split, fused_final=final is not None),
        grid=(t // tm, n_e // te),
        in_specs=in_specs,
        out_specs=out_specs,
        out_shape=out_shape,
        scratch_shapes=scratch,
        compiler_params=_cparams(("arbitrary", "arbitrary")),
        name="peer_dense",
    )(*args)


def _resid_kernel(x_ref, pt_ref, g2_ref, o_ref):
    o_ref[0] = x_ref[0] + g2_ref[0] * pt_ref[...].T


def _resid(x1, peer_t, g2, tm=256):
    bsz, s_len, d = x1.shape
    n_t = s_len // tm
    return pl.pallas_call(
        _resid_kernel,
        grid=(bsz, n_t),
        in_specs=[pl.BlockSpec((1, tm, d), lambda b, i: (b, i, 0)),
                  pl.BlockSpec((d, tm), lambda b, i: (0, b * n_t + i)),
                  pl.BlockSpec((1, 1, d), lambda b, i: (b, 0, 0))],
        out_specs=pl.BlockSpec((1, tm, d), lambda b, i: (b, i, 0)),
        out_shape=jax.ShapeDtypeStruct((bsz, s_len, d), F32),
        compiler_params=_cparams(("arbitrary", "arbitrary")),
        name="peer_residual",
    )(x1, peer_t, g2)


def _peer_layer(ht, h8, rs, w_q, k1, k2, u_tabs, u_unscale, u_rownorm, v_tabs, v_unscale, layer, final=None):
    w_s = _compose_score_weights(w_q, k1, k2)
    cnt, p1, r2, p2 = _route(ht, w_s)
    bound = jnp.maximum(PEER_HEADS * u_rownorm[layer] * rs[1:2], FP8_TINY)
    w_scale = jnp.exp2(jnp.floor(jnp.log2(FP8_TARGET / bound)))
    scales = jnp.concatenate([rs[0:1] * u_unscale[layer], w_scale, v_unscale[layer] / w_scale], axis=0)
    return _peer(h8, scales, u_tabs, v_tabs, layer, cnt, p1, r2, p2, final)


def _rope_tables(s_len):
    rows = s_len // GRID_W
    row = jnp.repeat(jnp.arange(rows, dtype=F32), GRID_W)
    col = jnp.tile(jnp.arange(GRID_W, dtype=F32), rows)
    n_pair_axis = HEAD_DIM // 4
    inv_freq = ROPE_THETA ** (-jnp.arange(n_pair_axis, dtype=F32) / n_pair_axis)
    ang = jnp.concatenate([row[:, None] * inv_freq, col[:, None] * inv_freq], axis=-1)
    cos, sin = jnp.cos(ang), jnp.sin(ang)
    return jnp.concatenate([cos, cos], axis=-1), jnp.concatenate([-sin, sin], axis=-1)


def _deinterleave_perm():
    half = HEAD_DIM // 2
    return np.concatenate([2 * np.arange(half), 2 * np.arange(half) + 1])


def kernel(x, c, ada_w, ada_b, norm_g, even_w_in, even_w_out, s5_lam_re, s5_lam_im, s5_log_dt, s5_b_re, s5_b_im, s5_c_re, s5_c_im, s5_d, s5_glu_w, s5_glu_b, q_norm_g, k_norm_g, odd_w_in, odd_w_out, sgu_ln_g, sgu_w, sgu_b, peer_w_q, peer_k1, peer_k2, peer_u, peer_v, final_g):
    bsz, s_len, d = x.shape
    depth = ada_w.shape[0]
    mod = _modulation(c, ada_w, ada_b)
    fg = final_g.reshape(1, d)
    n_a = s5_lam_re.shape[2] * S5_GROUP
    n_q = N_Q_HEADS * HEAD_DIM
    n_kv = N_KV_HEADS * HEAD_DIM
    n_steps = max(1, int(math.log2(s_len // S5_CHUNK)))
    cos_t, sin_t = _rope_tables(s_len)
    perm = _deinterleave_perm()
    u_amax = jnp.maximum(jnp.max(jnp.abs(peer_u), axis=(1, 2)), FP8_TINY)
    u_tabs = (peer_u * (FP8_TARGET / u_amax)[:, None, None]).astype(FP8)
    u_unscale = u_amax * (1.0 / FP8_TARGET)
    u8sq = jnp.square(u_tabs.astype(F32))
    u_rownorm = jnp.sqrt(jnp.max(jnp.sum(u8sq, axis=2), axis=1)) * u_unscale
    v_amax = jnp.maximum(jnp.max(jnp.abs(peer_v), axis=(1, 2)), FP8_TINY)
    v_tabs = (peer_v * (FP8_TARGET / v_amax)[:, None, None]).astype(FP8)
    v_unscale = v_amax * (1.0 / FP8_TARGET)

    pending = None
    for layer in range(depth):
        sh1, sc1, g1, sh2, sc2, g2 = [m.reshape(bsz, 1, d) for m in jnp.split(mod[layer], 6, axis=-1)]
        ng1 = norm_g[layer, 0].reshape(1, d)
        ng2 = norm_g[layer, 1].reshape(1, d)
        i = layer // 2
        if layer % 2 == 0:
            if pending is not None:
                x = _resid(*pending)
            w_in = even_w_in[i]
            col = np.arange(w_in.shape[1])
            for hd in range(N_Q_HEADS + N_KV_HEADS):
                lo = n_a + hd * HEAD_DIM
                col[lo:lo + HEAD_DIM] = lo + perm
            w_in_p = w_in[:, col].astype(BF16)
            gq = (q_norm_g[i][perm] * (HEAD_DIM ** -0.5)).reshape(1, HEAD_DIM)
            gk = k_norm_g[i][perm].reshape(1, HEAD_DIM)
            ua, q, k, v = _front0(x, ng1, sc1, sh1, w_in_p, gq, gk, cos_t, sin_t, n_a, n_q, n_kv)
            weights = _s5_weights(s5_lam_re[i], s5_lam_im[i], s5_log_dt[i], s5_b_re[i], s5_b_im[i],
                                  s5_c_re[i], s5_c_im[i], s5_d[i], s5_glu_w[i], s5_glu_b[i], n_steps)
            a_out = _s5(ua, weights, n_steps)
            b_out = _attention(q, k, v)
            w_out = even_w_out[i].astype(BF16)
            x1, ht, h8, rs = _mid0(a_out, b_out, w_out[:n_a], w_out[n_a:], x, g1, ng2, sc2, sh2)
        else:
            x1, ht, h8, rs = _odd(*pending, ng1, sc1, sh1, odd_w_in[i].astype(BF16), sgu_ln_g[i],
                                  sgu_w[i].astype(BF16), sgu_b[i], odd_w_out[i].astype(BF16), g1, ng2, sc2, sh2)
        last = layer == depth - 1
        out = _peer_layer(ht, h8, rs, peer_w_q[layer], peer_k1[layer], peer_k2[layer],
                          u_tabs, u_unscale, u_rownorm, v_tabs, v_unscale, layer, (x1, g2, fg) if last else None)
        pending = None if last else (x1, out, g2)
    return out
```
